```python
import math
import jax, jax.numpy as jnp
from jax import lax
import numpy as np


D_MODEL = 1024
BATCH = 4
SEQ = 4096
DEPTH = 4
DEC_BATCH = 128
DEC_SEQ = 1
PAST_LEN = 2048
PAGE_SIZE = 128

GDN_HEADS = 8
GDN_DK = 64
GDN_DV = 64
GDN_CONV = 4
GDN_CHUNK = 64
GDN_QK = GDN_HEADS * GDN_DK
GDN_V = GDN_HEADS * GDN_DV
GDN_CONV_CH = 2 * GDN_QK + GDN_V
FOX_HEADS = 8
FOX_DH = 64
FOX_BLOCK = 128
FOX_W = FOX_HEADS * FOX_DH
N_BRANCH = 2
IN_DIM = GDN_CONV_CH + GDN_V + 2 * GDN_HEADS + 3 * FOX_W + FOX_HEADS + N_BRANCH * D_MODEL
D_FF = 2816
N_EXPERTS = 8
TOP_K = 2
E_FF = 1408
N_DENSE = (DEPTH + 1) // 2
N_MOE = DEPTH // 2
DEEPNORM_ALPHA = (2.0 * DEPTH) ** 0.25
DEEPNORM_BETA = (8.0 * DEPTH) ** -0.25
LN_EPS = 1e-5
RMS_EPS = 1e-6

kernel_name = 'hybrid_gdn_fox_deepnorm_decode_step'


def layer_norm(x, g, b):
    xf = x.astype(jnp.float32)
    mu = jnp.mean(xf, axis=-1, keepdims=True)
    var = jnp.mean(jnp.square(xf - mu), axis=-1, keepdims=True)
    return ((xf - mu) * lax.rsqrt(var + LN_EPS) * g + b).astype(x.dtype)


def l2_normalize(x):
    xf = x.astype(jnp.float32)
    return xf * lax.rsqrt(jnp.sum(xf * xf, axis=-1, keepdims=True) + RMS_EPS)


def split_projection(p):
    sizes = (GDN_CONV_CH, GDN_V, GDN_HEADS, GDN_HEADS, 3 * FOX_W, FOX_HEADS, N_BRANCH * D_MODEL)
    idx = [int(i) for i in np.cumsum(sizes)[:-1]]
    return jnp.split(p, idx, axis=-1)


def causal_depthwise_conv(prefix, x, w):
    xp = jnp.concatenate([prefix.astype(x.dtype), x], axis=1)
    t = x.shape[1]
    y = w[0] * xp[:, 0:t]
    for i in range(1, GDN_CONV):
        y = y + w[i] * xp[:, i:i + t]
    return y, xp[:, -(GDN_CONV - 1):]


def gdn_chunked(q, k, v, g, beta, state0):
    bn, t, h, _ = q.shape
    dv = v.shape[-1]
    n = t // GDN_CHUNK

    def blocks(a):
        a = a.reshape((bn, n, GDN_CHUNK, h) + a.shape[3:])
        return jnp.moveaxis(a, (1, 3), (0, 2))

    qc, kc, vc, gc, bc = blocks(q), blocks(k), blocks(v), blocks(g), blocks(beta)
    G = jnp.cumsum(gc, axis=-1)
    idx = jnp.arange(GDN_CHUNK)
    incl = idx[:, None] >= idx[None, :]
    strict = idx[:, None] > idx[None, :]
    diff = jnp.where(incl, G[..., :, None] - G[..., None, :], 0.0)
    decay = jnp.where(incl, jnp.exp(diff), 0.0)
    A = jnp.where(strict, bc[..., :, None] * jnp.einsum('nbhik,nbhjk->nbhij', kc, kc) * decay, 0.0)
    eye = jnp.broadcast_to(jnp.eye(GDN_CHUNK, dtype=A.dtype), A.shape)
    t_inv = lax.linalg.triangular_solve(A + eye, eye, left_side=True, lower=True, unit_diagonal=True)
    u_base = jnp.einsum('nbhij,nbhjv->nbhiv', t_inv, vc * bc[..., None])
    w_dec = jnp.einsum('nbhij,nbhjk->nbhik', t_inv, kc * (bc * jnp.exp(G))[..., None])
    qk = jnp.einsum('nbhik,nbhjk->nbhij', qc, kc) * decay
    q_dec = qc * jnp.exp(G)[..., None]
    k_dec = kc * jnp.exp(G[..., -1:] - G)[..., None]
    g_end = jnp.exp(G[..., -1])

    def step(S, xs):
        u_b, w_d, qk_n, q_d, k_d, ge = xs
        u = u_b - jnp.einsum('bhik,bhkv->bhiv', w_d, S)
        o = jnp.einsum('bhik,bhkv->bhiv', q_d, S) + jnp.einsum('bhij,bhjv->bhiv', qk_n, u)
        S = S * ge[..., None, None] + jnp.einsum('bhik,bhiv->bhkv', k_d, u)
        return S, o

    S, o = lax.scan(step, state0.astype(jnp.float32), (u_base, w_dec, qk, q_dec, k_dec, g_end))
    o = jnp.moveaxis(o, (0, 2), (1, 3)).reshape(bn, t, h, dv)
    return o, S


def gdn_recurrent(q, k, v, g, beta, state0):
    def step(S, xs):
        q_t, k_t, v_t, g_t, b_t = xs
        S = S * jnp.exp(g_t)[..., None, None]
        u = b_t[..., None] * (v_t - jnp.einsum('bhkv,bhk->bhv', S, k_t))
        S = S + jnp.einsum('bhk,bhv->bhkv', k_t, u)
        return S, jnp.einsum('bhkv,bhk->bhv', S, q_t)

    xs = tuple(jnp.moveaxis(a, 1, 0) for a in (q, k, v, g, beta))
    S, o = lax.scan(step, state0.astype(jnp.float32), xs)
    return jnp.moveaxis(o, 0, 1), S


def fox_prompt(q, k, v, logf):
    bn, t, h, dh = q.shape
    nb = t // FOX_BLOCK
    scale = FOX_DH ** -0.5
    cT = jnp.moveaxis(jnp.cumsum(logf, axis=1), 1, 2)
    qb = q.reshape(bn, nb, FOX_BLOCK, h, dh).swapaxes(0, 1)
    cb = jnp.moveaxis(cT.reshape(bn, h, nb, FOX_BLOCK), 2, 0)
    kpos = jnp.arange(t)

    def block(args):
        q_blk, c_blk, i = args
        s = jnp.einsum('bqhd,bkhd->bhqk', q_blk, k).astype(jnp.float32) * scale
        s = s + (c_blk[..., :, None] - cT[..., None, :])
        qpos = i * FOX_BLOCK + jnp.arange(FOX_BLOCK)
        s = jnp.where(qpos[:, None] >= kpos[None, :], s, -jnp.inf)
        p = jax.nn.softmax(s, axis=-1)
        return jnp.einsum('bhqk,bkhd->bqhd', p.astype(v.dtype), v)

    o = lax.map(block, (qb, cb, jnp.arange(nb)))
    return o.swapaxes(0, 1).reshape(bn, t, h * dh)


def fox_sample(q, k, v, logf, k_past, v_past, logf_past):
    bn, t, h, dh = q.shape
    p_len = k_past.shape[1]
    scale = FOX_DH ** -0.5
    c_new = jnp.moveaxis(jnp.cumsum(logf, axis=1), 1, 2)
    c_past = jnp.cumsum(logf_past.astype(jnp.float32), axis=1)
    suffix = jnp.moveaxis(c_past[:, -1:] - c_past, 1, 2)
    s_past = jnp.einsum('bqhd,bkhd->bhqk', q, k_past).astype(jnp.float32) * scale
    s_past = s_past + c_new[..., :, None] + suffix[..., None, :]
    s_new = jnp.einsum('bqhd,bkhd->bhqk', q, k).astype(jnp.float32) * scale
    s_new = s_new + c_new[..., :, None] - c_new[..., None, :]
    idx = jnp.arange(t)
    s_new = jnp.where(idx[:, None] >= idx[None, :], s_new, -jnp.inf)
    p = jax.nn.softmax(jnp.concatenate([s_past, s_new], axis=-1), axis=-1).astype(v.dtype)
    o = jnp.einsum('bhqk,bkhd->bqhd', p[..., :p_len], v_past) + jnp.einsum('bhqk,bkhd->bqhd', p[..., p_len:], v)
    return o.reshape(bn, t, h * dh)


def token_mixer(x, conv_prefix, gdn_state0, fox_past, w_in, conv_w, a_log, dt_bias, gdn_norm_w,
                fox_f_bias, w_branch_a, w_branch_b, w_out):
    bn, t = x.shape[:2]
    p = jnp.einsum('btd,de->bte', x, w_in)
    qkv_raw, z, b_raw, a_raw, fox_qkv, f_raw, gate_raw = split_projection(p)
    qkv, conv_state = causal_depthwise_conv(conv_prefix, qkv_raw, conv_w)
    qkv = jax.nn.silu(qkv)
    q, k, v = jnp.split(qkv, [GDN_QK, 2 * GDN_QK], axis=-1)
    q = l2_normalize(q.reshape(bn, t, GDN_HEADS, GDN_DK)) * (GDN_DK ** -0.5)
    k = l2_normalize(k.reshape(bn, t, GDN_HEADS, GDN_DK))
    v = v.reshape(bn, t, GDN_HEADS, GDN_DV).astype(jnp.float32)
    beta = jax.nn.sigmoid(b_raw.astype(jnp.float32))
    g = -jnp.exp(a_log.astype(jnp.float32)) * jax.nn.softplus(a_raw.astype(jnp.float32) + dt_bias)
    if fox_past is None:
        o_a, S = gdn_chunked(q, k, v, g, beta, gdn_state0)
    else:
        o_a, S = gdn_recurrent(q, k, v, g, beta, gdn_state0)
    o_a = o_a * lax.rsqrt(jnp.mean(o_a * o_a, axis=-1, keepdims=True) + RMS_EPS) * gdn_norm_w
    o_a = o_a * jax.nn.silu(z.reshape(bn, t, GDN_HEADS, GDN_DV).astype(jnp.float32))
    o_a = o_a.reshape(bn, t, GDN_V).astype(x.dtype)
    fq, fk, fv = [a.reshape(bn, t, FOX_HEADS, FOX_DH) for a in jnp.split(fox_qkv, 3, axis=-1)]
    logf = jax.nn.log_sigmoid(f_raw.astype(jnp.float32) + fox_f_bias)
    if fox_past is None:
        o_b = fox_prompt(fq, fk, fv, logf)
    else:
        o_b = fox_sample(fq, fk, fv, logf, *fox_past)
    g_a, g_b = jnp.split(jax.nn.sigmoid(gate_raw.astype(jnp.float32)).astype(x.dtype), 2, axis=-1)
    merged = g_a * (o_a @ w_branch_a) + g_b * (o_b @ w_branch_b)
    return merged @ w_out, (fk, fv, logf.astype(x.dtype), S.astype(x.dtype), conv_state)


def swiglu(x, wg, wu, wd):
    return (jax.nn.silu(x @ wg) * (x @ wu)) @ wd


def moe_swiglu(x, router, wg, wu, wd):
    shape = x.shape
    xt = x.reshape(-1, D_MODEL)
    logits = (xt @ router).astype(jnp.float32)
    top_v, top_i = lax.top_k(logits, TOP_K)
    gates = jax.nn.softmax(top_v, axis=-1)
    combine = jnp.einsum('nk,nke->ne', gates, jax.nn.one_hot(top_i, N_EXPERTS, dtype=jnp.float32)).astype(x.dtype)
    y = jnp.zeros_like(xt)
    for e in range(N_EXPERTS):
        y = y + combine[:, e:e + 1] * swiglu(xt, wg[e], wu[e], wd[e])
    return y.reshape(shape)


def setup_inputs(seed: int = 0) -> dict:
    key = jax.random.key(seed)
    ks = jax.random.split(key, 32)
    f32 = jnp.float32
    n_pages = PAST_LEN // PAGE_SIZE
    n_pool = (5 * DEC_BATCH * n_pages) // 4

    def nrm(k, shape, scale):
        return scale * jax.random.normal(k, shape, f32)

    x_prompt = nrm(ks[0], (BATCH, SEQ, D_MODEL), 1.0)
    x_sample = nrm(ks[1], (DEC_BATCH, DEC_SEQ, D_MODEL), 1.0)
    cache_k = nrm(ks[2], (DEPTH, n_pool, PAGE_SIZE, FOX_HEADS, FOX_DH), 1.0)
    cache_v = nrm(ks[3], (DEPTH, n_pool, PAGE_SIZE, FOX_HEADS, FOX_DH), 1.0)
    cache_logf = jax.nn.log_sigmoid(3.0 + jax.random.normal(ks[4], (DEPTH, n_pool, PAGE_SIZE, FOX_HEADS), f32))
    state_gdn = nrm(ks[5], (DEPTH, DEC_BATCH, GDN_HEADS, GDN_DK, GDN_DV), GDN_DK ** -0.5)
    state_conv = nrm(ks[6], (DEPTH, DEC_BATCH, GDN_CONV - 1, GDN_CONV_CH), 1.0)
    page_table = jax.random.permutation(ks[7], n_pool)[:DEC_BATCH * n_pages].reshape(DEC_BATCH, n_pages).astype(jnp.int32)

    w_in = nrm(ks[8], (DEPTH, D_MODEL, IN_DIM), D_MODEL ** -0.5)
    conv_w = nrm(ks[9], (DEPTH, GDN_CONV, GDN_CONV_CH), GDN_CONV ** -0.5)
    gdn_a_log = jnp.log(jax.random.uniform(ks[10], (DEPTH, GDN_HEADS), f32, 1.0, 16.0))
    dt = jnp.exp(jax.random.uniform(ks[11], (DEPTH, GDN_HEADS), f32, math.log(1e-3), math.log(1e-1)))
    gdn_dt_bias = dt + jnp.log(-jnp.expm1(-dt))
    gdn_norm_w = 1.0 + nrm(ks[12], (DEPTH, GDN_DV), 0.02)
    fox_f_bias = jnp.linspace(1.0, 6.0, FOX_HEADS, dtype=f32)[None, :] + nrm(ks[13], (DEPTH, FOX_HEADS), 0.1)
    w_branch_a = nrm(ks[14], (DEPTH, GDN_V, D_MODEL), DEEPNORM_BETA * GDN_V ** -0.5)
    w_branch_b = nrm(ks[15], (DEPTH, FOX_W, D_MODEL), DEEPNORM_BETA * FOX_W ** -0.5)
    w_out = nrm(ks[16], (DEPTH, D_MODEL, D_MODEL), DEEPNORM_BETA * D_MODEL ** -0.5)
    ln1_g = 1.0 + nrm(ks[17], (DEPTH, D_MODEL), 0.02)
    ln1_b = nrm(ks[18], (DEPTH, D_MODEL), 0.02)
    ln2_g = 1.0 + nrm(ks[19], (DEPTH, D_MODEL), 0.02)
    ln2_b = nrm(ks[20], (DEPTH, D_MODEL), 0.02)
    ffn_w_gate = nrm(ks[21], (N_DENSE, D_MODEL, D_FF), DEEPNORM_BETA * D_MODEL ** -0.5)
    ffn_w_up = nrm(ks[22], (N_DENSE, D_MODEL, D_FF), DEEPNORM_BETA * D_MODEL ** -0.5)
    ffn_w_down = nrm(ks[23], (N_DENSE, D_FF, D_MODEL), DEEPNORM_BETA * D_FF ** -0.5)
    moe_router = nrm(ks[24], (N_MOE, D_MODEL, N_EXPERTS), D_MODEL ** -0.5)
    moe_w_gate = nrm(ks[25], (N_MOE, N_EXPERTS, D_MODEL, E_FF), DEEPNORM_BETA * D_MODEL ** -0.5)
    moe_w_up = nrm(ks[26], (N_MOE, N_EXPERTS, D_MODEL, E_FF), DEEPNORM_BETA * D_MODEL ** -0.5)
    moe_w_down = nrm(ks[27], (N_MOE, N_EXPERTS, E_FF, D_MODEL), DEEPNORM_BETA * E_FF ** -0.5)
    return {'x_prompt': x_prompt, 'x_sample': x_sample, 'cache_k': cache_k, 'cache_v': cache_v,
            'cache_logf': cache_logf, 'state_gdn': state_gdn, 'state_conv': state_conv,
            'page_table': page_table, 'w_in': w_in, 'conv_w': conv_w, 'gdn_a_log': gdn_a_log,
            'gdn_dt_bias': gdn_dt_bias, 'gdn_norm_w': gdn_norm_w, 'fox_f_bias': fox_f_bias,
            'w_branch_a': w_branch_a, 'w_branch_b': w_branch_b, 'w_out': w_out,
            'ln1_g': ln1_g, 'ln1_b': ln1_b, 'ln2_g': ln2_g, 'ln2_b': ln2_b,
            'ffn_w_gate': ffn_w_gate, 'ffn_w_up': ffn_w_up, 'ffn_w_down': ffn_w_down,
            'moe_router': moe_router, 'moe_w_gate': moe_w_gate, 'moe_w_up': moe_w_up,
            'moe_w_down': moe_w_down}


def reference(x_prompt, x_sample, cache_k, cache_v, cache_logf, state_gdn, state_conv, page_table,
              w_in, conv_w, gdn_a_log, gdn_dt_bias, gdn_norm_w, fox_f_bias, w_branch_a, w_branch_b,
              w_out, ln1_g, ln1_b, ln2_g, ln2_b, ffn_w_gate, ffn_w_up, ffn_w_down,
              moe_router, moe_w_gate, moe_w_up, moe_w_down):
    bp, sp = x_prompt.shape[:2]
    db = x_sample.shape[0]
    n_pages = page_table.shape[1]
    past_len = n_pages * PAGE_SIZE
    xp, xs = x_prompt, x_sample
    st_prompt, st_sample = [], []
    for l in range(DEPTH):
        mix_w = (w_in[l], conv_w[l], gdn_a_log[l], gdn_dt_bias[l], gdn_norm_w[l], fox_f_bias[l],
                 w_branch_a[l], w_branch_b[l], w_out[l])
        conv0 = jnp.zeros((bp, GDN_CONV - 1, GDN_CONV_CH), xp.dtype)
        s0 = jnp.zeros((bp, GDN_HEADS, GDN_DK, GDN_DV), jnp.float32)
        mp, sp_l = token_mixer(xp, conv0, s0, None, *mix_w)
        k_past = cache_k[l][page_table].reshape(db, past_len, FOX_HEADS, FOX_DH)
        v_past = cache_v[l][page_table].reshape(db, past_len, FOX_HEADS, FOX_DH)
        lf_past = cache_logf[l][page_table].reshape(db, past_len, FOX_HEADS)
        ms, ss_l = token_mixer(xs, state_conv[l], state_gdn[l], (k_past, v_past, lf_past), *mix_w)
        xp = layer_norm(DEEPNORM_ALPHA * xp + mp, ln1_g[l], ln1_b[l])
        xs = layer_norm(DEEPNORM_ALPHA * xs + ms, ln1_g[l], ln1_b[l])
        i = l // 2
        if l % 2 == 0:
            fp = swiglu(xp, ffn_w_gate[i], ffn_w_up[i], ffn_w_down[i])
            fs = swiglu(xs, ffn_w_gate[i], ffn_w_up[i], ffn_w_down[i])
        else:
            fp = moe_swiglu(xp, moe_router[i], moe_w_gate[i], moe_w_up[i], moe_w_down[i])
            fs = moe_swiglu(xs, moe_router[i], moe_w_gate[i], moe_w_up[i], moe_w_down[i])
        xp = layer_norm(DEEPNORM_ALPHA * xp + fp, ln2_g[l], ln2_b[l])
        xs = layer_norm(DEEPNORM_ALPHA * xs + fs, ln2_g[l], ln2_b[l])
        st_prompt.append(sp_l)
        st_sample.append(ss_l)
    n_prompt_pages = sp // PAGE_SIZE
    k_prompt = jnp.stack([s[0] for s in st_prompt]).reshape(DEPTH, bp, n_prompt_pages, PAGE_SIZE, FOX_HEADS, FOX_DH)
    v_prompt = jnp.stack([s[1] for s in st_prompt]).reshape(DEPTH, bp, n_prompt_pages, PAGE_SIZE, FOX_HEADS, FOX_DH)
    logf_prompt = jnp.stack([s[2] for s in st_prompt]).reshape(DEPTH, bp, n_prompt_pages, PAGE_SIZE, FOX_HEADS)
    gdn_prompt = jnp.stack([s[3] for s in st_prompt])
    conv_prompt = jnp.stack([s[4] for s in st_prompt])
    k_sample = jnp.stack([s[0] for s in st_sample])
    v_sample = jnp.stack([s[1] for s in st_sample])
    logf_sample = jnp.stack([s[2] for s in st_sample])
    gdn_sample = jnp.stack([s[3] for s in st_sample])
    conv_sample = jnp.stack([s[4] for s in st_sample])
    return (xp, xs, k_prompt, v_prompt, logf_prompt, gdn_prompt, conv_prompt,
            k_sample, v_sample, logf_sample, gdn_sample, conv_sample)
```

```python
import functools

import numpy as np
import jax
import jax.numpy as jnp
from jax import lax
from jax.experimental import pallas as pl
from jax.experimental.pallas import tpu as pltpu

F32 = jnp.float32
BF16 = jnp.bfloat16

D_MODEL = 1024
HEADS = 8
DH = 64
HW = HEADS * DH
CONV_K = 4
CONV_CH = 3 * HW
CHUNK = 64
PAGE = 128
N_EXPERTS = 8
FF_TILE = 1408
LANES = 128
N_PAIR = HEADS // 2

DEPTH = 4
ALPHA = (2.0 * DEPTH) ** 0.25
LN_EPS = 1e-5
RMS_EPS = 1e-6

C_GDN = 0
C_GATE = 2 * D_MODEL
C_FOX = 4 * D_MODEL
C_SMALL = C_FOX + 3 * HW
N_PROJ = C_SMALL + LANES
L_B, L_A, L_F = 0, 8, 16

VMEM_LIMIT = 56 * 1024 * 1024


def _cp(sem, vmem=VMEM_LIMIT):
    return pltpu.CompilerParams(dimension_semantics=sem, vmem_limit_bytes=vmem)


def _bdot(a, b):
    return jnp.dot(a.astype(BF16), b.astype(BF16), preferred_element_type=F32)


def _bdot_nt(a, b):
    return lax.dot_general(a.astype(BF16), b.astype(BF16), (((1,), (1,)), ((), ())),
                           preferred_element_type=F32)


def _bdot_tn(a, b):
    return lax.dot_general(a.astype(BF16), b.astype(BF16), (((0,), (0,)), ((), ())),
                           preferred_element_type=F32)


def _split3(x):
    x1 = x.astype(BF16)
    r1 = x - x1.astype(F32)
    x2 = r1.astype(BF16)
    x3 = (r1 - x2.astype(F32)).astype(BF16)
    return x1, x2, x3


def _sel_dot(sel, x):
    out = None
    for p in _split3(x):
        t = jnp.dot(sel, p, preferred_element_type=F32)
        out = t if out is None else out + t
    return out


def _dot_sel(x, sel):
    out = None
    for p in _split3(x):
        t = jnp.dot(p, sel, preferred_element_type=F32)
        out = t if out is None else out + t
    return out


def _sel_dot_nt(sel, x):
    out = None
    for p in _split3(x):
        t = lax.dot_general(sel, p, (((1,), (1,)), ((), ())), preferred_element_type=F32)
        out = t if out is None else out + t
    return out


def _silu(x):
    return x * jax.nn.sigmoid(x)


def _softplus(x):
    return jnp.maximum(x, 0.0) + jnp.log1p(jnp.exp(-jnp.abs(x)))


def _log_sigmoid(x):
    return jnp.minimum(x, 0.0) - jnp.log1p(jnp.exp(-jnp.abs(x)))


def _layer_norm(h, g, b):
    mu = jnp.mean(h, axis=-1, keepdims=True)
    d = h - mu
    var = jnp.mean(d * d, axis=-1, keepdims=True)
    return d * lax.rsqrt(var + LN_EPS) * g + b


def _stack_heads(x):
    lo = lax.broadcasted_iota(jnp.int32, x.shape, 1) < DH
    return jnp.concatenate([jnp.where(lo, x, 0.0), jnp.where(lo, 0.0, x)], axis=0)


def _fold_heads(x):
    r = x.shape[0] // 2
    return x[:r] + x[r:]


def _proj_kernel(x_ref, w_ref, o_ref):
    o_ref[...] = jnp.dot(x_ref[...].astype(BF16), w_ref[...], preferred_element_type=F32)


def _proj(x, w, tm):
    m, d = x.shape
    n = w.shape[1]
    tn = 1920
    return pl.pallas_call(
        _proj_kernel,
        grid=(n // tn, m // tm),
        in_specs=[pl.BlockSpec((tm, d), lambda j, i: (i, 0)),
                  pl.BlockSpec((d, tn), lambda j, i: (0, j))],
        out_specs=pl.BlockSpec((tm, tn), lambda j, i: (i, j)),
        out_shape=jax.ShapeDtypeStruct((m, n), F32),
        compiler_params=_cp(("arbitrary", "arbitrary")),
        name="proj",
    )(x, w)


def _pair_rsqrt_sum(x):
    lo = lax.broadcasted_iota(jnp.int32, x.shape, 1) < DH
    sq = x * x
    s0 = jnp.sum(jnp.where(lo, sq, 0.0), axis=-1, keepdims=True)
    s1 = jnp.sum(jnp.where(lo, 0.0, sq), axis=-1, keepdims=True)
    return jnp.where(lo, lax.rsqrt(s0 + RMS_EPS), lax.rsqrt(s1 + RMS_EPS))


def _gated_norm(o, z, nw):
    lo = lax.broadcasted_iota(jnp.int32, o.shape, 1) < DH
    sq = o * o
    m0 = jnp.sum(jnp.where(lo, sq, 0.0), axis=-1, keepdims=True) * (1.0 / DH)
    m1 = jnp.sum(jnp.where(lo, 0.0, sq), axis=-1, keepdims=True) * (1.0 / DH)
    r = jnp.where(lo, lax.rsqrt(m0 + RMS_EPS), lax.rsqrt(m1 + RMS_EPS))
    return o * r * nw * _silu(z)


def _gdn_prompt_kernel(p_ref, sm_ref, cw_ref, alog_ref, dtb_ref, nw_ref,
                       o_ref, s_out_ref,
                       xbuf, qkv_s, st_s, *, tb):
    t = pl.program_id(1)
    nt = pl.num_programs(1)

    @pl.when(t == 0)
    def _():
        xbuf[0:8, :] = jnp.zeros((8, CONV_CH), F32)
        st_s[...] = jnp.zeros(st_s.shape, F32)

    raw = p_ref[:, 0:CONV_CH]
    xbuf[8:8 + tb, :] = raw
    cw = cw_ref[...]
    y = cw[0:1] * xbuf[5:5 + tb, :] + cw[1:2] * xbuf[6:6 + tb, :]
    y = y + cw[2:3] * xbuf[7:7 + tb, :]
    y = y + cw[3:4] * raw
    xbuf[0:8, :] = raw[tb - 8:tb, :]
    y = _silu(y)
    for p in range(N_PAIR):
        qp = y[:, p * LANES:(p + 1) * LANES]
        qkv_s[:, p * LANES:(p + 1) * LANES] = qp * _pair_rsqrt_sum(qp) * (DH ** -0.5)
        kp = y[:, HW + p * LANES:HW + (p + 1) * LANES]
        qkv_s[:, HW + p * LANES:HW + (p + 1) * LANES] = kp * _pair_rsqrt_sum(kp)
    qkv_s[:, 2 * HW:] = y[:, 2 * HW:]

    c2 = 2 * CHUNK
    ri = lax.broadcasted_iota(jnp.int32, (c2, c2), 0)
    ci = lax.broadcasted_iota(jnp.int32, (c2, c2), 1)
    same = (ri >= CHUNK) == (ci >= CHUNK)
    ti = jnp.where(ri >= CHUNK, ri - CHUNK, ri)
    tj = jnp.where(ci >= CHUNK, ci - CHUNK, ci)
    incl = jnp.logical_and(same, ti >= tj)
    strict = jnp.logical_and(same, ti > tj)
    lr = lax.broadcasted_iota(jnp.int32, (CHUNK, CHUNK), 0)
    lc = lax.broadcasted_iota(jnp.int32, (CHUNK, CHUNK), 1)
    ltri = jnp.where(lr >= lc, 1.0, 0.0).astype(BF16)
    alog = alog_ref[...]
    dtb = dtb_ref[...]
    nw = nw_ref[...]

    def chunk_body(c, carry):
        c0 = pl.multiple_of(c * CHUNK, CHUNK)
        sm = sm_ref[pl.ds(c0, CHUNK), :]
        beta_all = jax.nn.sigmoid(sm)
        g_all = -jnp.exp(alog) * _softplus(sm + dtb)
        gcum = _sel_dot(ltri, g_all)
        for p in range(N_PAIR):
            h0, h1 = 2 * p, 2 * p + 1
            sl = slice(p * LANES, (p + 1) * LANES)

            def colb(a, lane0, lane1):
                return jnp.concatenate([jnp.broadcast_to(a[:, lane0:lane0 + 1], (CHUNK, LANES)),
                                        jnp.broadcast_to(a[:, lane1:lane1 + 1], (CHUNK, LANES))], axis=0)

            bb = colb(beta_all, L_B + h0, L_B + h1)
            cb = colb(gcum, L_A + h0, L_A + h1)
            glast = jnp.concatenate(
                [jnp.broadcast_to(gcum[CHUNK - 1:CHUNK, L_A + h0:L_A + h0 + 1], (CHUNK, LANES)),
                 jnp.broadcast_to(gcum[CHUNK - 1:CHUNK, L_A + h1:L_A + h1 + 1], (CHUNK, LANES))], axis=0)
            rb = cb.T
            decay = jnp.where(incl, jnp.exp(jnp.where(incl, cb - rb, 0.0)), 0.0)
            eg = jnp.exp(cb)

            kst = _stack_heads(qkv_s[pl.ds(c0, CHUNK), HW + p * LANES:HW + (p + 1) * LANES])
            qst = _stack_heads(qkv_s[pl.ds(c0, CHUNK), sl])
            vst = _stack_heads(qkv_s[pl.ds(c0, CHUNK), 2 * HW + p * LANES:2 * HW + (p + 1) * LANES])
            kst_b = kst.astype(BF16)
            kk = _bdot_nt(kst_b, kst_b)
            a = jnp.where(strict, bb * kk * decay, 0.0)
            x = -a
            pm = x
            for _ in range(5):
                x = _bdot(x, x)
                pm = pm + x + _bdot(pm, x)
            rhs = jnp.concatenate([vst * bb, kst * (bb * eg)], axis=1)
            tr = rhs + _bdot(pm, rhs)
            u_base = _fold_heads(tr[:, :LANES])
            w_dec = _fold_heads(tr[:, LANES:])
            qk = _bdot_nt(qst, kst_b) * decay
            q_dec = _fold_heads(qst * eg)
            kdec_st = kst * jnp.exp(glast - cb)

            s_prev = st_s[p]
            qws = _bdot(jnp.concatenate([q_dec, w_dec], axis=0), s_prev)
            u = u_base - qws[CHUNK:]
            ust = _stack_heads(u)
            o = qws[:CHUNK] + _fold_heads(_bdot(qk, ust))
            st_s[p] = s_prev * jnp.exp(glast) + _bdot_tn(kdec_st, ust)

            z = p_ref[pl.ds(c0, CHUNK), CONV_CH + p * LANES:CONV_CH + (p + 1) * LANES]
            o_ref[pl.ds(c0, CHUNK), sl] = _gated_norm(o, z, nw)
        return carry

    lax.fori_loop(0, tb // CHUNK, chunk_body, 0)

    @pl.when(t == nt - 1)
    def _():
        for p in range(N_PAIR):
            s = st_s[p]
            s_out_ref[2 * p] = s[:DH, :DH]
            s_out_ref[2 * p + 1] = s[DH:, DH:]


def _gdn_prompt(pp, batch, seq, conv_w, alog_row, dtb_row, nw_row, tb=512):
    nt = seq // tb
    kern = functools.partial(_gdn_prompt_kernel, tb=tb)
    return pl.pallas_call(
        kern,
        grid=(batch, nt),
        in_specs=[pl.BlockSpec((tb, 2 * D_MODEL), lambda b, t: (b * nt + t, 0)),
                  pl.BlockSpec((tb, LANES), lambda b, t: (b * nt + t, C_SMALL // LANES)),
                  pl.BlockSpec((CONV_K, CONV_CH), lambda b, t: (0, 0)),
                  pl.BlockSpec((1, LANES), lambda b, t: (0, 0)),
                  pl.BlockSpec((1, LANES), lambda b, t: (0, 0)),
                  pl.BlockSpec((1, LANES), lambda b, t: (0, 0))],
        out_specs=[pl.BlockSpec((tb, HW), lambda b, t: (b * nt + t, 0)),
                   pl.BlockSpec((None, HEADS, DH, DH), lambda b, t: (b, 0, 0, 0))],
        out_shape=[jax.ShapeDtypeStruct((batch * seq, HW), F32),
                   jax.ShapeDtypeStruct((batch, HEADS, DH, DH), F32)],
        scratch_shapes=[pltpu.VMEM((tb + 8, CONV_CH), F32),
                        pltpu.VMEM((tb, CONV_CH), F32),
                        pltpu.VMEM((N_PAIR, LANES, LANES), F32)],
        compiler_params=_cp(("arbitrary", "arbitrary")),
        name="gdn_prompt",
    )(pp, pp, conv_w, alog_row, dtb_row, nw_row)


def _gdn_sample_kernel(p_ref, sm_ref, sc_ref, st_ref, cw_ref, alog_ref, dtb_ref, nw_ref,
                       o_ref, s_out_ref, obuf, *, bs):
    raw = p_ref[:, 0:CONV_CH]
    cw = cw_ref[...]
    y = cw[0:1] * sc_ref[0] + cw[1:2] * sc_ref[1]
    y = y + cw[2:3] * sc_ref[2]
    y = y + cw[3:4] * raw
    y = _silu(y)
    qs, ks = [], []
    for p in range(N_PAIR):
        qp = y[:, p * LANES:(p + 1) * LANES]
        qs.append(qp * _pair_rsqrt_sum(qp) * (DH ** -0.5))
        kp = y[:, HW + p * LANES:HW + (p + 1) * LANES]
        ks.append(kp * _pair_rsqrt_sum(kp))
    q = jnp.concatenate(qs, axis=1)
    k = jnp.concatenate(ks, axis=1)
    v = y[:, 2 * HW:]
    ri = lax.broadcasted_iota(jnp.int32, (HW, HW), 0)
    ci = lax.broadcasted_iota(jnp.int32, (HW, HW), 1)
    eye = jnp.where(ri == ci, 1.0, 0.0).astype(BF16)
    q_t = _sel_dot_nt(eye, q)
    k_t = _sel_dot_nt(eye, k)
    sm = sm_ref[...]
    beta = jax.nn.sigmoid(sm)
    dec = jnp.exp(-jnp.exp(alog_ref[...]) * _softplus(sm + dtb_ref[...]))
    for i in range(bs):
        for h in range(HEADS):
            s = st_ref[i, h]
            kc = k_t[h * DH:(h + 1) * DH, i:i + 1]
            qc = q_t[h * DH:(h + 1) * DH, i:i + 1]
            vr = v[i:i + 1, h * DH:(h + 1) * DH]
            s = s * dec[i:i + 1, L_A + h:L_A + h + 1]
            sk = jnp.sum(s * kc, axis=0, keepdims=True)
            u = beta[i:i + 1, L_B + h:L_B + h + 1] * (vr - sk)
            s = s + kc * u
            s_out_ref[i, h] = s
            obuf[i:i + 1, h * DH:(h + 1) * DH] = jnp.sum(s * qc, axis=0, keepdims=True)
    nw = nw_ref[...]
    for p in range(N_PAIR):
        sl = slice(p * LANES, (p + 1) * LANES)
        z = p_ref[:, CONV_CH + p * LANES:CONV_CH + (p + 1) * LANES]
        o_ref[:, sl] = _gated_norm(obuf[:, sl], z, nw)


def _gdn_sample(ps, state_conv_t, state_gdn, layer, conv_w, alog_row, dtb_row, nw_row, bs=8):
    nb = ps.shape[0]
    kern = functools.partial(_gdn_sample_kernel, bs=bs)
    return pl.pallas_call(
        kern,
        grid=(nb // bs,),
        in_specs=[pl.BlockSpec((bs, 2 * D_MODEL), lambda i: (i, 0)),
                  pl.BlockSpec((bs, LANES), lambda i: (i, C_SMALL // LANES)),
                  pl.BlockSpec((CONV_K - 1, bs, CONV_CH), lambda i: (0, i, 0)),
                  pl.BlockSpec((None, bs, HEADS, DH, DH), lambda i: (layer, i, 0, 0, 0)),
                  pl.BlockSpec((CONV_K, CONV_CH), lambda i: (0, 0)),
                  pl.BlockSpec((1, LANES), lambda i: (0, 0)),
                  pl.BlockSpec((1, LANES), lambda i: (0, 0)),
                  pl.BlockSpec((1, LANES), lambda i: (0, 0))],
        out_specs=[pl.BlockSpec((bs, HW), lambda i: (i, 0)),
                   pl.BlockSpec((bs, HEADS, DH, DH), lambda i: (i, 0, 0, 0))],
        out_shape=[jax.ShapeDtypeStruct((nb, HW), F32),
                   jax.ShapeDtypeStruct((nb, HEADS, DH, DH), F32)],
        scratch_shapes=[pltpu.VMEM((bs, HW), F32)],
        compiler_params=_cp(("arbitrary",)),
        name="gdn_sample",
    )(ps, ps, state_conv_t, state_gdn, conv_w, alog_row, dtb_row, nw_row)


def _fox_prep_kernel(sm_ref, fb_ref, lf_ref, cc_ref, cr_ref, *, seq):
    blk = LANES
    ri = lax.broadcasted_iota(jnp.int32, (blk, blk), 0)
    ci = lax.broadcasted_iota(jnp.int32, (blk, blk), 1)
    ltri = jnp.where(ri >= ci, 1.0, 0.0).astype(BF16)
    r32 = lax.broadcasted_iota(jnp.int32, (4 * 8, blk), 0)
    c32 = lax.broadcasted_iota(jnp.int32, (4 * 8, blk), 1)
    pr, jr = r32 // 8, r32 % 8
    selt = jnp.where(jnp.logical_and(jr < 2, c32 == L_F + 2 * pr + jr), 1.0, 0.0).astype(BF16)
    fb = fb_ref[...]
    carry = jnp.zeros((1, blk), F32)
    for i in range(seq // blk):
        lf = _log_sigmoid(sm_ref[i * blk:(i + 1) * blk, :] + fb)
        lf_ref[i * blk:(i + 1) * blk, :] = lf
        c = _sel_dot(ltri, lf) + carry
        cc_ref[i * blk:(i + 1) * blk, :] = c
        cr_ref[i] = _sel_dot_nt(selt, c)
        carry = c[blk - 1:blk, :]


def _fox_prep(pp, batch, seq, fb_row):
    nk = seq // LANES
    kern = functools.partial(_fox_prep_kernel, seq=seq)
    return pl.pallas_call(
        kern,
        grid=(batch,),
        in_specs=[pl.BlockSpec((seq, LANES), lambda b: (b, C_SMALL // LANES)),
                  pl.BlockSpec((1, LANES), lambda b: (0, 0))],
        out_specs=[pl.BlockSpec((seq, LANES), lambda b: (b, 0)),
                   pl.BlockSpec((seq, LANES), lambda b: (b, 0)),
                   pl.BlockSpec((None, nk, 4 * 8, LANES), lambda b: (b, 0, 0, 0))],
        out_shape=[jax.ShapeDtypeStruct((batch * seq, LANES), F32),
                   jax.ShapeDtypeStruct((batch * seq, LANES), F32),
                   jax.ShapeDtypeStruct((batch, nk, 4 * 8, LANES), F32)],
        compiler_params=_cp(("arbitrary",)),
        name="fox_prep",
    )(pp, fb_row)


def _fox_prompt_kernel(q_ref, k_ref, v_ref, cc_ref, cr_ref, o_ref, m_s, l_s, acc_s, *, tq):
    hp = pl.program_id(1)
    qi = pl.program_id(2)
    lane = lax.broadcasted_iota(jnp.int32, (tq, LANES), 1)
    lo = lane < DH
    q = q_ref[...] * (DH ** -0.5)
    qm = [jnp.where(lo, q, 0.0).astype(BF16), jnp.where(lo, 0.0, q).astype(BF16)]
    cc = cc_ref[...]
    cq = [jnp.sum(jnp.where(lane == L_F + 2 * hp + j, cc, 0.0), axis=-1, keepdims=True) for j in range(2)]
    m_s[...] = jnp.full(m_s.shape, -jnp.inf, F32)
    l_s[...] = jnp.zeros(l_s.shape, F32)
    acc_s[...] = jnp.zeros(acc_s.shape, F32)
    sub = tq // LANES
    rowi = lax.broadcasted_iota(jnp.int32, (tq, tq), 0)
    coli = lax.broadcasted_iota(jnp.int32, (tq, tq), 1)
    causal = rowi >= coli

    def step(j, masked):
        j0 = pl.multiple_of(j * tq, tq)
        kb = k_ref[pl.ds(j0, tq), :].astype(BF16)
        vb = v_ref[pl.ds(j0, tq), :].astype(BF16)
        for h in range(2):
            s = _bdot_nt(qm[h], kb)
            ck = jnp.concatenate([cr_ref[j * sub + u, h:h + 1, :] for u in range(sub)], axis=1)
            s = s + (cq[h] - ck)
            if masked:
                s = jnp.where(causal, s, -jnp.inf)
            m_old = m_s[h]
            m_new = jnp.maximum(m_old, jnp.max(s, axis=-1, keepdims=True))
            alpha = jnp.exp(m_old - m_new)
            pexp = jnp.exp(s - m_new)
            l_s[h] = alpha * l_s[h] + jnp.sum(pexp, axis=-1, keepdims=True)
            acc_s[h] = alpha * acc_s[h] + jnp.dot(pexp.astype(BF16), vb, preferred_element_type=F32)
            m_s[h] = m_new

    def body(j, carry):
        step(j, False)
        return carry

    lax.fori_loop(0, qi, body, 0)
    step(qi, True)
    o_ref[...] = jnp.where(lo, acc_s[0] / l_s[0], acc_s[1] / l_s[1])


def _fox_prompt(pp, cc, cr, batch, seq, tq=512):
    nq = seq // tq
    nk = seq // LANES
    kern = functools.partial(_fox_prompt_kernel, tq=tq)
    cq0 = C_FOX // LANES
    return pl.pallas_call(
        kern,
        grid=(batch, N_PAIR, nq),
        in_specs=[pl.BlockSpec((tq, LANES), lambda b, p, i: (b * nq + i, cq0 + p)),
                  pl.BlockSpec((seq, LANES), lambda b, p, i: (b, cq0 + N_PAIR + p)),
                  pl.BlockSpec((seq, LANES), lambda b, p, i: (b, cq0 + 2 * N_PAIR + p)),
                  pl.BlockSpec((tq, LANES), lambda b, p, i: (b * nq + i, 0)),
                  pl.BlockSpec((None, nk, 8, LANES), lambda b, p, i: (b, 0, p, 0))],
        out_specs=pl.BlockSpec((tq, LANES), lambda b, p, i: (b * nq + i, p)),
        out_shape=jax.ShapeDtypeStruct((batch * seq, HW), F32),
        scratch_shapes=[pltpu.VMEM((2, tq, 1), F32), pltpu.VMEM((2, tq, 1), F32),
                        pltpu.VMEM((2, tq, LANES), F32)],
        compiler_params=_cp(("arbitrary", "arbitrary", "arbitrary")),
        name="fox_prompt",
    )(pp, pp, pp, cc, cr)


def _fox_sample_kernel(pt_ref, lay_ref, q_ref, k_ref, v_ref, sm_ref, fb_ref, *rest, n_pages):
    kp = rest[0:n_pages]
    vp = rest[n_pages:2 * n_pages]
    lp = rest[2 * n_pages:3 * n_pages]
    o_ref, lf_ref = rest[3 * n_pages], rest[3 * n_pages + 1]
    del pt_ref, lay_ref
    past = n_pages * PAGE

    r8 = lax.broadcasted_iota(jnp.int32, (HEADS, HW), 0)
    c8 = lax.broadcasted_iota(jnp.int32, (HEADS, HW), 1)
    own = (c8 // DH) == r8
    fr = lax.broadcasted_iota(jnp.int32, (HW, DH), 0)
    fc = lax.broadcasted_iota(jnp.int32, (HW, DH), 1)
    fold = jnp.where(fr % DH == fc, 1.0, 0.0).astype(BF16)

    def heads_form(row):
        return _dot_sel(jnp.where(own, jnp.broadcast_to(row, (HEADS, HW)), 0.0), fold)

    q = heads_form(q_ref[...]) * (DH ** -0.5)
    kn = heads_form(k_ref[...])
    vn = heads_form(v_ref[...])

    sm = sm_ref[...]
    lf_new = _log_sigmoid(sm + fb_ref[...])
    lf_ref[...] = lf_new
    e8r = lax.broadcasted_iota(jnp.int32, (HEADS, LANES), 0)
    e8c = lax.broadcasted_iota(jnp.int32, (HEADS, LANES), 1)
    pick_f = jnp.where(e8c == e8r + L_F, 1.0, 0.0).astype(BF16)
    c_new = _sel_dot_nt(pick_f, jnp.broadcast_to(lf_new, (HEADS, LANES)))[:, 0:1]

    i8r = lax.broadcasted_iota(jnp.int32, (HEADS, HEADS), 0)
    i8c = lax.broadcasted_iota(jnp.int32, (HEADS, HEADS), 1)
    eye8 = jnp.where(i8r == i8c, 1.0, 0.0).astype(BF16)
    ur = lax.broadcasted_iota(jnp.int32, (PAGE, PAGE), 0)
    uc = lax.broadcasted_iota(jnp.int32, (PAGE, PAGE), 1)
    upper = jnp.where(ur > uc, 1.0, 0.0).astype(BF16)
    suffix = [None] * n_pages
    tail = jnp.zeros((HEADS, 1), F32)
    for j in reversed(range(n_pages)):
        lrow = _sel_dot_nt(eye8, lp[j][...].astype(F32))
        suffix[j] = _dot_sel(lrow, upper) + tail
        tail = tail + jnp.sum(lrow, axis=-1, keepdims=True)
    bias = jnp.concatenate(suffix, axis=1) + c_new

    qb = q.astype(BF16)
    hrow = lax.broadcasted_iota(jnp.int32, (HEADS, past), 0)
    s = jnp.zeros((HEADS, past), F32)
    for h in range(HEADS):
        kh = jnp.concatenate([kp[j][:, h, :] for j in range(n_pages)], axis=0)
        sh = lax.dot_general(qb, kh.astype(BF16), (((1,), (1,)), ((), ())), preferred_element_type=F32)
        s = jnp.where(hrow == h, sh, s)
    s = s + bias
    s_new = jnp.sum(q.astype(BF16).astype(F32) * kn.astype(BF16).astype(F32), axis=-1, keepdims=True)
    m = jnp.maximum(jnp.max(s, axis=-1, keepdims=True), s_new)
    pe = jnp.exp(s - m)
    pn = jnp.exp(s_new - m)
    den = jnp.sum(pe, axis=-1, keepdims=True) + pn
    pe = (pe / den).astype(BF16)
    pn = pn / den
    h64 = lax.broadcasted_iota(jnp.int32, (HEADS, DH), 0)
    o = pn.astype(BF16).astype(F32) * vn.astype(BF16).astype(F32)
    for h in range(HEADS):
        vh = jnp.concatenate([vp[j][:, h, :] for j in range(n_pages)], axis=0)
        oh = jnp.dot(pe, vh.astype(BF16), preferred_element_type=F32)
        o = o + jnp.where(h64 == h, oh, 0.0)
    o_ref[...] = o


def _fox_sample(ps3, page_table, layer_arr, cache_k, cache_v, cache_logf, fb_row):
    nb = ps3.shape[0]
    n_pages = page_table.shape[1]
    kern = functools.partial(_fox_sample_kernel, n_pages=n_pages)
    cq0 = C_FOX // HW

    def page_spec(j, tail):
        nz = (0,) * len(tail)
        return pl.BlockSpec((None, None, PAGE) + tail, lambda b, pt, lay: (lay[0], pt[b, j], 0) + nz)

    in_specs = [pl.BlockSpec((None, 1, HW), lambda b, pt, lay: (b, 0, cq0)),
                pl.BlockSpec((None, 1, HW), lambda b, pt, lay: (b, 0, cq0 + 1)),
                pl.BlockSpec((None, 1, HW), lambda b, pt, lay: (b, 0, cq0 + 2)),
                pl.BlockSpec((None, 1, LANES), lambda b, pt, lay: (b, 0, C_SMALL // LANES)),
                pl.BlockSpec((1, LANES), lambda b, pt, lay: (0, 0))]
    in_specs += [page_spec(j, (HEADS, DH)) for j in range(n_pages)]
    in_specs += [page_spec(j, (HEADS, DH)) for j in range(n_pages)]
    in_specs += [page_spec(j, (HEADS,)) for j in range(n_pages)]
    grid_spec = pltpu.PrefetchScalarGridSpec(
        num_scalar_prefetch=2,
        grid=(nb,),
        in_specs=in_specs,
        out_specs=[pl.BlockSpec((None, HEADS, DH), lambda b, pt, lay: (b, 0, 0)),
                   pl.BlockSpec((None, 1, LANES), lambda b, pt, lay: (b, 0, 0))])
    return pl.pallas_call(
        kern,
        grid_spec=grid_spec,
        out_shape=[jax.ShapeDtypeStruct((nb, HEADS, DH), F32),
                   jax.ShapeDtypeStruct((nb, 1, LANES), F32)],
        compiler_params=_cp(("arbitrary",)),
        name="fox_sample",
    )(page_table, layer_arr, ps3, ps3, ps3, ps3, fb_row,
      *([cache_k] * n_pages), *([cache_v] * n_pages), *([cache_logf] * n_pages))


def _merge_kernel(x_ref, oa_ref, ob_ref, g_ref, wa_ref, wb_ref, wo_ref, lg_ref, lb_ref, y_ref):
    ga = jax.nn.sigmoid(g_ref[:, :D_MODEL])
    gb = jax.nn.sigmoid(g_ref[:, D_MODEL:])
    a = jnp.dot(oa_ref[...].astype(BF16), wa_ref[...], preferred_element_type=F32)
    b = jnp.dot(ob_ref[...].astype(BF16), wb_ref[...], preferred_element_type=F32)
    merged = ga * a + gb * b
    mix = jnp.dot(merged.astype(BF16), wo_ref[...], preferred_element_type=F32)
    y_ref[...] = _layer_norm(ALPHA * x_ref[...] + mix, lg_ref[...], lb_ref[...])


def _merge(x, oa, ob, pp, wa, wb, wo, lg, lb, tm):
    m = x.shape[0]
    const = lambda i: (0, 0)
    return pl.pallas_call(
        _merge_kernel,
        grid=(m // tm,),
        in_specs=[pl.BlockSpec((tm, D_MODEL), lambda i: (i, 0)),
                  pl.BlockSpec((tm, HW), lambda i: (i, 0)),
                  pl.BlockSpec((tm, HW), lambda i: (i, 0)),
                  pl.BlockSpec((tm, 2 * D_MODEL), lambda i: (i, C_GATE // (2 * D_MODEL))),
                  pl.BlockSpec((HW, D_MODEL), const),
                  pl.BlockSpec((HW, D_MODEL), const),
                  pl.BlockSpec((D_MODEL, D_MODEL), const),
                  pl.BlockSpec((1, D_MODEL), const),
                  pl.BlockSpec((1, D_MODEL), const)],
        out_specs=pl.BlockSpec((tm, D_MODEL), lambda i: (i, 0)),
        out_shape=jax.ShapeDtypeStruct((m, D_MODEL), F32),
        compiler_params=_cp(("arbitrary",)),
        name="merge_ln",
    )(x, oa, ob, pp, wa, wb, wo, lg, lb)


def _router_kernel(x_ref, r_ref, c_ref):
    logits = jnp.dot(x_ref[...], r_ref[...], preferred_element_type=F32,
                     precision=lax.Precision.HIGHEST)
    lane = lax.broadcasted_iota(jnp.int32, logits.shape, 1).astype(F32)
    neg = -jnp.inf
    lg = jnp.where(lane < N_EXPERTS, logits, neg)
    m1 = jnp.max(lg, axis=-1, keepdims=True)
    i1 = jnp.min(jnp.where(lg == m1, lane, float(LANES)), axis=-1, keepdims=True)
    lg2 = jnp.where(lane == i1, neg, lg)
    m2 = jnp.max(lg2, axis=-1, keepdims=True)
    i2 = jnp.min(jnp.where(lg2 == m2, lane, float(LANES)), axis=-1, keepdims=True)
    e2 = jnp.exp(m2 - m1)
    den = 1.0 + e2
    c_ref[...] = jnp.where(lane == i1, 1.0 / den, 0.0) + jnp.where(lane == i2, e2 / den, 0.0)


def _router(x, r_pad, tm):
    m = x.shape[0]
    return pl.pallas_call(
        _router_kernel,
        grid=(m // tm,),
        in_specs=[pl.BlockSpec((tm, D_MODEL), lambda i: (i, 0)),
                  pl.BlockSpec((D_MODEL, LANES), lambda i: (0, 0))],
        out_specs=pl.BlockSpec((tm, LANES), lambda i: (i, 0)),
        out_shape=jax.ShapeDtypeStruct((m, LANES), F32),
        compiler_params=_cp(("arbitrary",)),
        name="router",
    )(x, r_pad)


def _ffn_kernel(x_ref, c_ref, wg_ref, wu_ref, wd_ref, lg_ref, lb_ref, y_ref, acc_ref, *, weighted):
    s = pl.program_id(1)
    ns = pl.num_programs(1)
    xb = x_ref[...].astype(BF16)
    hg = jnp.dot(xb, wg_ref[...], preferred_element_type=F32)
    hu = jnp.dot(xb, wu_ref[...], preferred_element_type=F32)
    d = jnp.dot((_silu(hg) * hu).astype(BF16), wd_ref[...], preferred_element_type=F32)
    if weighted:
        c = c_ref[...]
        lane = lax.broadcasted_iota(jnp.int32, c.shape, 1)
        d = jnp.sum(jnp.where(lane == s, c, 0.0), axis=-1, keepdims=True) * d

    @pl.when(s == 0)
    def _():
        acc_ref[...] = d

    @pl.when(s > 0)
    def _():
        acc_ref[...] = acc_ref[...] + d

    @pl.when(s == ns - 1)
    def _():
        y_ref[...] = _layer_norm(ALPHA * x_ref[...] + acc_ref[...], lg_ref[...], lb_ref[...])


def _ffn(x, comb, wg, wu, wd, lg, lb, tm, weighted):
    m = x.shape[0]
    ns = wg.shape[0]
    kern = functools.partial(_ffn_kernel, weighted=weighted)
    return pl.pallas_call(
        kern,
        grid=(m // tm, ns),
        in_specs=[pl.BlockSpec((tm, D_MODEL), lambda i, s: (i, 0)),
                  pl.BlockSpec((tm, LANES), lambda i, s: (i, 0)),
                  pl.BlockSpec((None, D_MODEL, FF_TILE), lambda i, s: (s, 0, 0)),
                  pl.BlockSpec((None, D_MODEL, FF_TILE), lambda i, s: (s, 0, 0)),
                  pl.BlockSpec((None, FF_TILE, D_MODEL), lambda i, s: (s, 0, 0)),
                  pl.BlockSpec((1, D_MODEL), lambda i, s: (0, 0)),
                  pl.BlockSpec((1, D_MODEL), lambda i, s: (0, 0))],
        out_specs=pl.BlockSpec((tm, D_MODEL), lambda i, s: (i, 0)),
        out_shape=jax.ShapeDtypeStruct((m, D_MODEL), F32),
        scratch_shapes=[pltpu.VMEM((tm, D_MODEL), F32)],
        compiler_params=_cp(("arbitrary", "arbitrary")),
        name="ffn_ln",
    )(x, comb, wg, wu, wd, lg, lb)


def _lane_row(vec, offset):
    return jnp.zeros((1, LANES), F32).at[0, offset:offset + vec.shape[0]].set(vec.astype(F32))


def kernel(x_prompt, x_sample, cache_k, cache_v, cache_logf, state_gdn, state_conv, page_table,
           w_in, conv_w, gdn_a_log, gdn_dt_bias, gdn_norm_w, fox_f_bias, w_branch_a, w_branch_b,
           w_out, ln1_g, ln1_b, ln2_g, ln2_b, ffn_w_gate, ffn_w_up, ffn_w_down,
           moe_router, moe_w_gate, moe_w_up, moe_w_down):
    bp, sp, d = x_prompt.shape
    db = x_sample.shape[0]
    depth = w_in.shape[0]
    n_pages = page_table.shape[1]
    mp = bp * sp
    tm_p, tm_s = 512, db

    xp = x_prompt.reshape(mp, d)
    xs = x_sample.reshape(db, d)
    state_conv_t = jnp.swapaxes(state_conv, 1, 2)

    o_qkv = CONV_CH + HW
    o_fox = o_qkv + 2 * HEADS
    o_f = o_fox + 3 * HW
    o_gate = o_f + HEADS

    outs = {k: [] for k in ("kp", "vp", "lfp", "gp", "cp", "ks", "vs", "lfs", "gs", "cs")}
    for l in range(depth):
        w = w_in[l]
        w_proj = jnp.concatenate(
            [w[:, 0:o_qkv], w[:, o_gate:o_gate + 2 * d], w[:, o_fox:o_fox + 3 * HW],
             w[:, o_qkv:o_qkv + 2 * HEADS], w[:, o_f:o_f + HEADS],
             jnp.zeros((d, LANES - 3 * HEADS), F32)], axis=1).astype(BF16)
        wa, wb, wo = w_branch_a[l].astype(BF16), w_branch_b[l].astype(BF16), w_out[l].astype(BF16)
        alog_row = _lane_row(gdn_a_log[l], L_A)
        dtb_row = _lane_row(gdn_dt_bias[l], L_A)
        fb_row = _lane_row(fox_f_bias[l], L_F)
        nw_row = jnp.tile(gdn_norm_w[l].astype(F32), 2).reshape(1, LANES)
        cw = conv_w[l]
        lg1, lb1 = ln1_g[l].reshape(1, d), ln1_b[l].reshape(1, d)
        lg2, lb2 = ln2_g[l].reshape(1, d), ln2_b[l].reshape(1, d)

        pp = _proj(xp, w_proj, 1024)
        oa_p, s_p = _gdn_prompt(pp, bp, sp, cw, alog_row, dtb_row, nw_row)
        lf_p, cc_p, cr_p = _fox_prep(pp, bp, sp, fb_row)
        ob_p = _fox_prompt(pp, cc_p, cr_p, bp, sp)
        x1p = _merge(xp, oa_p, ob_p, pp, wa, wb, wo, lg1, lb1, tm_p)

        ps = _proj(xs, w_proj, tm_s)
        oa_s, s_s = _gdn_sample(ps, state_conv_t[l], state_gdn, l, cw, alog_row, dtb_row, nw_row)
        ob_s3, lf_s = _fox_sample(ps.reshape(db, 1, N_PROJ), page_table,
                                  jnp.full((1,), l, jnp.int32), cache_k, cache_v, cache_logf, fb_row)
        x1s = _merge(xs, oa_s, ob_s3.reshape(db, HW), ps, wa, wb, wo, lg1, lb1, tm_s)

        i = l // 2
        if l % 2 == 0:
            def tiles(wm):
                return jnp.swapaxes(wm.reshape(d, -1, FF_TILE), 0, 1).astype(BF16)
            wg, wu = tiles(ffn_w_gate[i]), tiles(ffn_w_up[i])
            wd = ffn_w_down[i].reshape(-1, FF_TILE, d).astype(BF16)
            cmb_p = jnp.zeros((mp, LANES), F32)
            cmb_s = jnp.zeros((db, LANES), F32)
            weighted = False
        else:
            wg, wu, wd = (moe_w_gate[i].astype(BF16), moe_w_up[i].astype(BF16),
                          moe_w_down[i].astype(BF16))
            r_pad = jnp.concatenate([moe_router[i], jnp.zeros((d, LANES - N_EXPERTS), F32)], axis=1)
            cmb_p = _router(x1p, r_pad, tm_p)
            cmb_s = _router(x1s, r_pad, tm_s)
            weighted = True
        xp = _ffn(x1p, cmb_p, wg, wu, wd, lg2, lb2, tm_p, weighted)
        xs = _ffn(x1s, cmb_s, wg, wu, wd, lg2, lb2, tm_s, weighted)

        npp = sp // PAGE
        kcol = C_FOX + HW
        outs["kp"].append(pp[:, kcol:kcol + HW].reshape(bp, npp, PAGE, HEADS, DH))
        outs["vp"].append(pp[:, kcol + HW:kcol + 2 * HW].reshape(bp, npp, PAGE, HEADS, DH))
        outs["lfp"].append(lf_p[:, L_F:L_F + HEADS].reshape(bp, npp, PAGE, HEADS))
        outs["gp"].append(s_p)
        outs["cp"].append(pp.reshape(bp, sp, N_PROJ)[:, sp - (CONV_K - 1):, 0:CONV_CH])
        outs["ks"].append(ps[:, kcol:kcol + HW].reshape(db, 1, HEADS, DH))
        outs["vs"].append(ps[:, kcol + HW:kcol + 2 * HW].reshape(db, 1, HEADS, DH))
        outs["lfs"].append(lf_s[:, :, L_F:L_F + HEADS])
        outs["gs"].append(s_s)
        outs["cs"].append(jnp.concatenate([state_conv[l][:, 1:], ps[:, None, 0:CONV_CH]], axis=1))

    st = {k: jnp.stack(v) for k, v in outs.items()}
    return (xp.reshape(bp, sp, d), xs.reshape(db, 1, d),
            st["kp"], st["vp"], st["lfp"], st["gp"], st["cp"],
            st["ks"], st["vs"], st["lfs"], st["gs"], st["cs"])
```

```python
import functools

import numpy as np
import jax
import jax.numpy as jnp
from jax import lax
from jax.experimental import pallas as pl
from jax.experimental.pallas import tpu as pltpu

F32 = jnp.float32
BF16 = jnp.bfloat16

D_MODEL = 1024
HEADS = 8
DH = 64
HW = HEADS * DH
CONV_K = 4
CONV_CH = 3 * HW
CHUNK = 64
PAGE = 128
N_EXPERTS = 8
FF_TILE = 1408
LANES = 128
N_PAIR = HEADS // 2
N_SPLIT = 1

DEPTH = 4
ALPHA = (2.0 * DEPTH) ** 0.25
LN_EPS = 1e-5
RMS_EPS = 1e-6
LOG2E = 1.4426950408889634

C_GDN = 0
C_GATE = 2 * D_MODEL
C_FOX = 4 * D_MODEL
C_SMALL = C_FOX + 3 * HW
N_PROJ = C_SMALL + LANES
L_B, L_A, L_F = 0, 8, 16

VMEM_LIMIT = 56 * 1024 * 1024


def _cp(sem, vmem=VMEM_LIMIT):
    return pltpu.CompilerParams(dimension_semantics=sem, vmem_limit_bytes=vmem)


def _bdot(a, b):
    return jnp.dot(a.astype(BF16), b.astype(BF16), preferred_element_type=F32)


def _bdot_nt(a, b):
    return lax.dot_general(a.astype(BF16), b.astype(BF16), (((1,), (1,)), ((), ())),
                           preferred_element_type=F32)


def _bdot_tn(a, b):
    return lax.dot_general(a.astype(BF16), b.astype(BF16), (((0,), (0,)), ((), ())),
                           preferred_element_type=F32)


def _split3(x):
    x1 = x.astype(BF16)
    r1 = x - x1.astype(F32)
    x2 = r1.astype(BF16)
    x3 = (r1 - x2.astype(F32)).astype(BF16)
    return x1, x2, x3


def _sel_dot(sel, x):
    out = None
    for p in _split3(x):
        t = jnp.dot(sel, p, preferred_element_type=F32)
        out = t if out is None else out + t
    return out


def _dot_sel(x, sel):
    out = None
    for p in _split3(x):
        t = jnp.dot(p, sel, preferred_element_type=F32)
        out = t if out is None else out + t
    return out


def _sel_dot_nt(sel, x):
    out = None
    for p in _split3(x):
        t = lax.dot_general(sel, p, (((1,), (1,)), ((), ())), preferred_element_type=F32)
        out = t if out is None else out + t
    return out


def _silu(x):
    return x * jax.nn.sigmoid(x)


def _softplus(x):
    return jnp.maximum(x, 0.0) + jnp.log1p(jnp.exp(-jnp.abs(x)))


def _log_sigmoid(x):
    return jnp.minimum(x, 0.0) - jnp.log1p(jnp.exp(-jnp.abs(x)))


def _layer_norm(h, g, b):
    mu = jnp.mean(h, axis=-1, keepdims=True)
    d = h - mu
    var = jnp.mean(d * d, axis=-1, keepdims=True)
    return d * lax.rsqrt(var + LN_EPS) * g + b


def _stack_heads(x):
    lo = lax.broadcasted_iota(jnp.int32, x.shape, 1) < DH
    return jnp.concatenate([jnp.where(lo, x, 0.0), jnp.where(lo, 0.0, x)], axis=0)


def _fold_heads(x):
    r = x.shape[0] // 2
    return x[:r] + x[r:]


def _proj_kernel(x_ref, w_ref, o_ref):
    o_ref[...] = jnp.dot(x_ref[...].astype(BF16), w_ref[...], preferred_element_type=F32)


def _proj(x, w, tm):
    m, d = x.shape
    n = w.shape[1]
    tn = 1920
    return pl.pallas_call(
        _proj_kernel,
        grid=(n // tn, m // tm),
        in_specs=[pl.BlockSpec((tm, d), lambda j, i: (i, 0)),
                  pl.BlockSpec((d, tn), lambda j, i: (0, j))],
        out_specs=pl.BlockSpec((tm, tn), lambda j, i: (i, j)),
        out_shape=jax.ShapeDtypeStruct((m, n), F32),
        compiler_params=_cp(("arbitrary", "arbitrary")),
        name="proj",
    )(x, w)


def _pair_rsqrt_sum(x):
    lo = lax.broadcasted_iota(jnp.int32, x.shape, 1) < DH
    sq = x * x
    s0 = jnp.sum(jnp.where(lo, sq, 0.0), axis=-1, keepdims=True)
    s1 = jnp.sum(jnp.where(lo, 0.0, sq), axis=-1, keepdims=True)
    return jnp.where(lo, lax.rsqrt(s0 + RMS_EPS), lax.rsqrt(s1 + RMS_EPS))


def _gated_norm(o, z, nw):
    lo = lax.broadcasted_iota(jnp.int32, o.shape, 1) < DH
    sq = o * o
    m0 = jnp.sum(jnp.where(lo, sq, 0.0), axis=-1, keepdims=True) * (1.0 / DH)
    m1 = jnp.sum(jnp.where(lo, 0.0, sq), axis=-1, keepdims=True) * (1.0 / DH)
    r = jnp.where(lo, lax.rsqrt(m0 + RMS_EPS), lax.rsqrt(m1 + RMS_EPS))
    return o * r * nw * _silu(z)


def _gdn_prompt_kernel(p_ref, sm_ref, cw_ref, alog_ref, dtb_ref, nw_ref,
                       o_ref, s_out_ref,
                       xbuf, qkv_s, st_s, *, tb):
    t = pl.program_id(1)
    nt = pl.num_programs(1)

    @pl.when(t == 0)
    def _():
        xbuf[0:8, :] = jnp.zeros((8, CONV_CH), F32)
        st_s[...] = jnp.zeros(st_s.shape, F32)

    raw = p_ref[:, 0:CONV_CH]
    xbuf[8:8 + tb, :] = raw
    cw = cw_ref[...]
    y = cw[0:1] * xbuf[5:5 + tb, :] + cw[1:2] * xbuf[6:6 + tb, :]
    y = y + cw[2:3] * xbuf[7:7 + tb, :]
    y = y + cw[3:4] * raw
    xbuf[0:8, :] = raw[tb - 8:tb, :]
    y = _silu(y)
    for p in range(N_PAIR):
        qp = y[:, p * LANES:(p + 1) * LANES]
        qkv_s[:, p * LANES:(p + 1) * LANES] = qp * _pair_rsqrt_sum(qp) * (DH ** -0.5)
        kp = y[:, HW + p * LANES:HW + (p + 1) * LANES]
        qkv_s[:, HW + p * LANES:HW + (p + 1) * LANES] = kp * _pair_rsqrt_sum(kp)
    qkv_s[:, 2 * HW:] = y[:, 2 * HW:]

    c2 = 2 * CHUNK
    ri = lax.broadcasted_iota(jnp.int32, (c2, c2), 0)
    ci = lax.broadcasted_iota(jnp.int32, (c2, c2), 1)
    same = (ri >= CHUNK) == (ci >= CHUNK)
    ti = jnp.where(ri >= CHUNK, ri - CHUNK, ri)
    tj = jnp.where(ci >= CHUNK, ci - CHUNK, ci)
    incl = jnp.logical_and(same, ti >= tj)
    strict = jnp.logical_and(same, ti > tj)
    lr = lax.broadcasted_iota(jnp.int32, (CHUNK, CHUNK), 0)
    lc = lax.broadcasted_iota(jnp.int32, (CHUNK, CHUNK), 1)
    ltri = jnp.where(lr >= lc, 1.0, 0.0).astype(BF16)
    alog = alog_ref[...]
    dtb = dtb_ref[...]
    nw = nw_ref[...]

    def chunk_body(c, carry):
        c0 = pl.multiple_of(c * CHUNK, CHUNK)
        sm = sm_ref[pl.ds(c0, CHUNK), :]
        beta_all = jax.nn.sigmoid(sm)
        g_all = -jnp.exp(alog) * _softplus(sm + dtb)
        gcum = _sel_dot(ltri, g_all)
        def colb(a, lane0, lane1):
            return jnp.concatenate([jnp.broadcast_to(a[:, lane0:lane0 + 1], (CHUNK, LANES)),
                                    jnp.broadcast_to(a[:, lane1:lane1 + 1], (CHUNK, LANES))], axis=0)

        loaded = []
        for p in range(N_PAIR):
            loaded.append((qkv_s[pl.ds(c0, CHUNK), p * LANES:(p + 1) * LANES],
                           qkv_s[pl.ds(c0, CHUNK), HW + p * LANES:HW + (p + 1) * LANES],
                           qkv_s[pl.ds(c0, CHUNK), 2 * HW + p * LANES:2 * HW + (p + 1) * LANES],
                           p_ref[pl.ds(c0, CHUNK), CONV_CH + p * LANES:CONV_CH + (p + 1) * LANES],
                           st_s[p]))
        pairs = range(N_PAIR)
        bb, cb, glast = [], [], []
        for p in pairs:
            h0, h1 = 2 * p, 2 * p + 1
            bb.append(colb(beta_all, L_B + h0, L_B + h1))
            cb.append(colb(gcum, L_A + h0, L_A + h1))
            glast.append(jnp.concatenate(
                [jnp.broadcast_to(gcum[CHUNK - 1:CHUNK, L_A + h0:L_A + h0 + 1], (CHUNK, LANES)),
                 jnp.broadcast_to(gcum[CHUNK - 1:CHUNK, L_A + h1:L_A + h1 + 1], (CHUNK, LANES))], axis=0))
        decay = [jnp.where(incl, jnp.exp(jnp.where(incl, cb[p] - cb[p].T, 0.0)), 0.0) for p in pairs]
        eg = [jnp.exp(cb[p]) for p in pairs]
        qst = [_stack_heads(loaded[p][0]) for p in pairs]
        kst = [_stack_heads(loaded[p][1]) for p in pairs]
        vst = [_stack_heads(loaded[p][2]) for p in pairs]
        kst_b = [kst[p].astype(BF16) for p in pairs]
        kk = [_bdot_nt(kst_b[p], kst_b[p]) for p in pairs]
        x = [-jnp.where(strict, bb[p] * kk[p] * decay[p], 0.0) for p in pairs]
        pm = x
        for _ in range(5):
            x = [_bdot(x[p], x[p]) for p in pairs]
            px = [_bdot(pm[p], x[p]) for p in pairs]
            pm = [pm[p] + x[p] + px[p] for p in pairs]
        rhs = [jnp.concatenate([vst[p] * bb[p], kst[p] * (bb[p] * eg[p])], axis=1) for p in pairs]
        tr = [rhs[p] + _bdot(pm[p], rhs[p]) for p in pairs]
        qk = [_bdot_nt(qst[p], kst_b[p]) * decay[p] for p in pairs]
        qw = [jnp.concatenate([_fold_heads(qst[p] * eg[p]), _fold_heads(tr[p][:, LANES:])], axis=0)
              for p in pairs]
        qws = [_bdot(qw[p], loaded[p][4]) for p in pairs]
        ust = [_stack_heads(_fold_heads(tr[p][:, :LANES]) - qws[p][CHUNK:]) for p in pairs]
        o = [qws[p][:CHUNK] + _fold_heads(_bdot(qk[p], ust[p])) for p in pairs]
        s_new = [loaded[p][4] * jnp.exp(glast[p]) + _bdot_tn(kst[p] * jnp.exp(glast[p] - cb[p]), ust[p])
                 for p in pairs]
        results = [(_gated_norm(o[p], loaded[p][3], nw), s_new[p]) for p in pairs]
        for p in range(N_PAIR):
            o_ref[pl.ds(c0, CHUNK), p * LANES:(p + 1) * LANES] = results[p][0]
            st_s[p] = results[p][1]
        return carry

    lax.fori_loop(0, tb // CHUNK, chunk_body, 0)

    @pl.when(t == nt - 1)
    def _():
        for p in range(N_PAIR):
            s = st_s[p]
            s_out_ref[2 * p] = s[:DH, :DH]
            s_out_ref[2 * p + 1] = s[DH:, DH:]


def _gdn_prompt(pp, batch, seq, conv_w, alog_row, dtb_row, nw_row, tb=512):
    nt = seq // tb
    kern = functools.partial(_gdn_prompt_kernel, tb=tb)
    return pl.pallas_call(
        kern,
        grid=(batch, nt),
        in_specs=[pl.BlockSpec((tb, 2 * D_MODEL), lambda b, t: (b * nt + t, 0)),
                  pl.BlockSpec((tb, LANES), lambda b, t: (b * nt + t, C_SMALL // LANES)),
                  pl.BlockSpec((CONV_K, CONV_CH), lambda b, t: (0, 0)),
                  pl.BlockSpec((1, LANES), lambda b, t: (0, 0)),
                  pl.BlockSpec((1, LANES), lambda b, t: (0, 0)),
                  pl.BlockSpec((1, LANES), lambda b, t: (0, 0))],
        out_specs=[pl.BlockSpec((tb, HW), lambda b, t: (b * nt + t, 0)),
                   pl.BlockSpec((None, HEADS, DH, DH), lambda b, t: (b, 0, 0, 0))],
        out_shape=[jax.ShapeDtypeStruct((batch * seq, HW), F32),
                   jax.ShapeDtypeStruct((batch, HEADS, DH, DH), F32)],
        scratch_shapes=[pltpu.VMEM((tb + 8, CONV_CH), F32),
                        pltpu.VMEM((tb, CONV_CH), F32),
                        pltpu.VMEM((N_PAIR, LANES, LANES), F32)],
        compiler_params=_cp(("arbitrary", "arbitrary")),
        name="gdn_prompt",
    )(pp, pp, conv_w, alog_row, dtb_row, nw_row)


def _gdn_sample_kernel(p_ref, sm_ref, sc_ref, st_ref, cw_ref, alog_ref, dtb_ref, nw_ref,
                       o_ref, s_out_ref, obuf, *, bs):
    raw = p_ref[:, 0:CONV_CH]
    cw = cw_ref[...]
    y = cw[0:1] * sc_ref[0] + cw[1:2] * sc_ref[1]
    y = y + cw[2:3] * sc_ref[2]
    y = y + cw[3:4] * raw
    y = _silu(y)
    qs, ks = [], []
    for p in range(N_PAIR):
        qp = y[:, p * LANES:(p + 1) * LANES]
        qs.append(qp * _pair_rsqrt_sum(qp) * (DH ** -0.5))
        kp = y[:, HW + p * LANES:HW + (p + 1) * LANES]
        ks.append(kp * _pair_rsqrt_sum(kp))
    q = jnp.concatenate(qs, axis=1)
    k = jnp.concatenate(ks, axis=1)
    v = y[:, 2 * HW:]
    ri = lax.broadcasted_iota(jnp.int32, (HW, HW), 0)
    ci = lax.broadcasted_iota(jnp.int32, (HW, HW), 1)
    eye = jnp.where(ri == ci, 1.0, 0.0).astype(BF16)
    q_t = _sel_dot_nt(eye, q)
    k_t = _sel_dot_nt(eye, k)
    sm = sm_ref[...]
    beta = jax.nn.sigmoid(sm)
    dec = jnp.exp(-jnp.exp(alog_ref[...]) * _softplus(sm + dtb_ref[...]))
    for i in range(bs):
        for h in range(HEADS):
            s = st_ref[i, h]
            kc = k_t[h * DH:(h + 1) * DH, i:i + 1]
            qc = q_t[h * DH:(h + 1) * DH, i:i + 1]
            vr = v[i:i + 1, h * DH:(h + 1) * DH]
            s = s * dec[i:i + 1, L_A + h:L_A + h + 1]
            sk = jnp.sum(s * kc, axis=0, keepdims=True)
            u = beta[i:i + 1, L_B + h:L_B + h + 1] * (vr - sk)
            s = s + kc * u
            s_out_ref[i, h] = s
            obuf[i:i + 1, h * DH:(h + 1) * DH] = jnp.sum(s * qc, axis=0, keepdims=True)
    nw = nw_ref[...]
    for p in range(N_PAIR):
        sl = slice(p * LANES, (p + 1) * LANES)
        z = p_ref[:, CONV_CH + p * LANES:CONV_CH + (p + 1) * LANES]
        o_ref[:, sl] = _gated_norm(obuf[:, sl], z, nw)


def _gdn_sample(ps, state_conv_t, state_gdn, layer, conv_w, alog_row, dtb_row, nw_row, bs=8):
    nb = ps.shape[0]
    kern = functools.partial(_gdn_sample_kernel, bs=bs)
    return pl.pallas_call(
        kern,
        grid=(nb // bs,),
        in_specs=[pl.BlockSpec((bs, 2 * D_MODEL), lambda i: (i, 0)),
                  pl.BlockSpec((bs, LANES), lambda i: (i, C_SMALL // LANES)),
                  pl.BlockSpec((CONV_K - 1, bs, CONV_CH), lambda i: (0, i, 0)),
                  pl.BlockSpec((None, bs, HEADS, DH, DH), lambda i: (layer, i, 0, 0, 0)),
                  pl.BlockSpec((CONV_K, CONV_CH), lambda i: (0, 0)),
                  pl.BlockSpec((1, LANES), lambda i: (0, 0)),
                  pl.BlockSpec((1, LANES), lambda i: (0, 0)),
                  pl.BlockSpec((1, LANES), lambda i: (0, 0))],
        out_specs=[pl.BlockSpec((bs, HW), lambda i: (i, 0)),
                   pl.BlockSpec((bs, HEADS, DH, DH), lambda i: (i, 0, 0, 0))],
        out_shape=[jax.ShapeDtypeStruct((nb, HW), F32),
                   jax.ShapeDtypeStruct((nb, HEADS, DH, DH), F32)],
        scratch_shapes=[pltpu.VMEM((bs, HW), F32)],
        compiler_params=_cp(("arbitrary",)),
        name="gdn_sample",
    )(ps, ps, state_conv_t, state_gdn, conv_w, alog_row, dtb_row, nw_row)


def _fox_prep_kernel(sm_ref, fb_ref, lf_ref, ck_ref, cr_ref, *, seq):
    blk = LANES
    ri = lax.broadcasted_iota(jnp.int32, (blk, blk), 0)
    ci = lax.broadcasted_iota(jnp.int32, (blk, blk), 1)
    ltri = jnp.where(ri >= ci, 1.0, 0.0).astype(BF16)
    r32 = lax.broadcasted_iota(jnp.int32, (4 * 8, blk), 0)
    c32 = lax.broadcasted_iota(jnp.int32, (4 * 8, blk), 1)
    pr, jr = r32 // 8, r32 % 8
    selt = jnp.where(jnp.logical_and(jr < 2, c32 == L_F + 2 * pr + jr), 1.0, 0.0).astype(BF16)
    fb = fb_ref[...]
    carry = jnp.zeros((1, blk), F32)
    for i in range(seq // blk):
        lf = _log_sigmoid(sm_ref[i * blk:(i + 1) * blk, :] + fb)
        lf_ref[i * blk:(i + 1) * blk, :] = lf
        c = _sel_dot(ltri, lf) + carry
        for h in range(HEADS):
            ck_ref[h, i * blk:(i + 1) * blk, :] = jnp.broadcast_to(c[:, L_F + h:L_F + h + 1] * LOG2E, (blk, blk))
        cr_ref[i] = _sel_dot_nt(selt, c)
        carry = c[blk - 1:blk, :]


def _fox_prep(pp, batch, seq, fb_row):
    nk = seq // LANES
    kern = functools.partial(_fox_prep_kernel, seq=seq)
    return pl.pallas_call(
        kern,
        grid=(batch,),
        in_specs=[pl.BlockSpec((seq, LANES), lambda b: (b, C_SMALL // LANES)),
                  pl.BlockSpec((1, LANES), lambda b: (0, 0))],
        out_specs=[pl.BlockSpec((seq, LANES), lambda b: (b, 0)),
                   pl.BlockSpec((None, HEADS, seq, LANES), lambda b: (b, 0, 0, 0)),
                   pl.BlockSpec((None, nk, 4 * 8, LANES), lambda b: (b, 0, 0, 0))],
        out_shape=[jax.ShapeDtypeStruct((batch * seq, LANES), F32),
                   jax.ShapeDtypeStruct((batch, HEADS, seq, LANES), F32),
                   jax.ShapeDtypeStruct((batch, nk, 4 * 8, LANES), F32)],
        compiler_params=_cp(("arbitrary",)),
        name="fox_prep",
    )(pp, fb_row)


def _fox_prompt_kernel(q_ref, k_ref, v_ref, ck_ref, cr_ref, o_ref, m_s, l_s, acc_s, *, tq):
    qi = pl.program_id(2)
    lane = lax.broadcasted_iota(jnp.int32, (tq, LANES), 1)
    lo = lane < DH
    q = q_ref[...] * (DH ** -0.5 * LOG2E)
    qm = [jnp.where(lo, q, 0.0).astype(BF16), jnp.where(lo, 0.0, q).astype(BF16)]
    sub = tq // LANES
    cq = [jnp.concatenate([cr_ref[qi * sub + u, h:h + 1, :] for u in range(sub)], axis=1) * LOG2E
          for h in range(2)]
    m_s[...] = jnp.full(m_s.shape, -jnp.inf, F32)
    l_s[...] = jnp.zeros(l_s.shape, F32)
    acc_s[...] = jnp.zeros(acc_s.shape, F32)
    cs = tq // N_SPLIT
    keyi = lax.broadcasted_iota(jnp.int32, (tq, cs), 0)
    qryi = lax.broadcasted_iota(jnp.int32, (tq, cs), 1)

    def step(j, masked):
        j0 = pl.multiple_of(j * tq, tq)
        kb = k_ref[pl.ds(j0, tq), :].astype(BF16)
        vt = v_ref[pl.ds(j0, tq), :].T.astype(BF16)
        ck = [ck_ref[h, pl.ds(j0, tq), :] for h in range(2)]
        units = [(h, r) for r in range(N_SPLIT) for h in range(2)]

        def scores(u):
            h, r = u
            cols = slice(r * cs, (r + 1) * cs)
            s = _bdot_nt(kb, qm[h][cols])
            s = s - jnp.concatenate([ck[h]] * (cs // LANES), axis=1)
            if masked:
                s = jnp.where(qryi + r * cs >= keyi, s, -jnp.inf)
            return s

        def softmax_pv(u, s):
            h, r = u
            cols = slice(r * cs, (r + 1) * cs)
            cqc = cq[h][:, cols]
            m_old = m_s[h, :, cols]
            m_new = jnp.maximum(m_old, jnp.max(s, axis=0, keepdims=True) + cqc)
            alpha = jnp.exp2(m_old - m_new)
            pexp = jnp.exp2(s - (m_new - cqc))
            l_s[h, :, cols] = alpha * l_s[h, :, cols] + jnp.sum(pexp, axis=0, keepdims=True)
            acc_s[h, :, cols] = alpha * acc_s[h, :, cols] + jnp.dot(vt, pexp.astype(BF16),
                                                                    preferred_element_type=F32)
            m_s[h, :, cols] = m_new

        s_next = scores(units[0])
        for i, u in enumerate(units):
            s_cur = s_next
            if i + 1 < len(units):
                s_next = scores(units[i + 1])
            softmax_pv(u, s_cur)

    def body(j, carry):
        step(j, False)
        return carry

    lax.fori_loop(0, qi, body, 0)
    step(qi, True)
    top = lax.broadcasted_iota(jnp.int32, (LANES, tq), 0) < DH
    o_t = jnp.where(top, acc_s[0] / l_s[0], acc_s[1] / l_s[1])
    o_ref[...] = o_t.T


def _fox_prompt(pp, ck, cr, batch, seq, tq=512):
    nq = seq // tq
    nk = seq // LANES
    kern = functools.partial(_fox_prompt_kernel, tq=tq)
    cq0 = C_FOX // LANES
    return pl.pallas_call(
        kern,
        grid=(batch, N_PAIR, nq),
        in_specs=[pl.BlockSpec((tq, LANES), lambda b, p, i: (b * nq + i, cq0 + p)),
                  pl.BlockSpec((seq, LANES), lambda b, p, i: (b, cq0 + N_PAIR + p)),
                  pl.BlockSpec((seq, LANES), lambda b, p, i: (b, cq0 + 2 * N_PAIR + p)),
                  pl.BlockSpec((None, 2, seq, LANES), lambda b, p, i: (b, p, 0, 0)),
                  pl.BlockSpec((None, nk, 8, LANES), lambda b, p, i: (b, 0, p, 0))],
        out_specs=pl.BlockSpec((tq, LANES), lambda b, p, i: (b * nq + i, p)),
        out_shape=jax.ShapeDtypeStruct((batch * seq, HW), F32),
        scratch_shapes=[pltpu.VMEM((2, 1, tq), F32), pltpu.VMEM((2, 1, tq), F32),
                        pltpu.VMEM((2, LANES, tq), F32)],
        compiler_params=_cp(("arbitrary", "arbitrary", "arbitrary")),
        name="fox_prompt",
    )(pp, pp, pp, ck, cr)


def _fox_sample_kernel(pt_ref, lay_ref, q_ref, k_ref, v_ref, sm_ref, fb_ref, *rest, n_pages):
    kp = rest[0:n_pages]
    vp = rest[n_pages:2 * n_pages]
    lp = rest[2 * n_pages:3 * n_pages]
    o_ref, lf_ref = rest[3 * n_pages], rest[3 * n_pages + 1]
    del pt_ref, lay_ref

    r8 = lax.broadcasted_iota(jnp.int32, (HEADS, HW), 0)
    c8 = lax.broadcasted_iota(jnp.int32, (HEADS, HW), 1)
    own = (c8 // DH) == r8
    fr = lax.broadcasted_iota(jnp.int32, (HW, DH), 0)
    fc = lax.broadcasted_iota(jnp.int32, (HW, DH), 1)
    fold = jnp.where(fr % DH == fc, 1.0, 0.0).astype(BF16)

    def heads_form(row):
        return _dot_sel(jnp.where(own, jnp.broadcast_to(row, (HEADS, HW)), 0.0), fold)

    q = heads_form(q_ref[...]) * (DH ** -0.5)
    kn = heads_form(k_ref[...])
    vn = heads_form(v_ref[...])
    er = lax.broadcasted_iota(jnp.int32, (DH, DH), 0)
    ec = lax.broadcasted_iota(jnp.int32, (DH, DH), 1)
    eye = jnp.where(er == ec, 1.0, 0.0).astype(BF16)
    q_t = _sel_dot_nt(eye, q)

    lf_new = _log_sigmoid(sm_ref[...] + fb_ref[...])
    lf_ref[...] = lf_new
    e8r = lax.broadcasted_iota(jnp.int32, (HEADS, LANES), 0)
    e8c = lax.broadcasted_iota(jnp.int32, (HEADS, LANES), 1)
    pick_f = jnp.where(e8c == e8r + L_F, 1.0, 0.0).astype(BF16)
    c_new = _sel_dot_nt(pick_f, jnp.broadcast_to(lf_new, (HEADS, LANES)))[:, 0:1]

    ur = lax.broadcasted_iota(jnp.int32, (PAGE, PAGE), 0)
    uc = lax.broadcasted_iota(jnp.int32, (PAGE, PAGE), 1)
    upper = jnp.where(ur > uc, 1.0, 0.0).astype(BF16)
    bias = [None] * n_pages
    tail = c_new
    for j in reversed(range(n_pages)):
        lrow = lp[j][...]
        bias[j] = _dot_sel(lrow, upper) + tail
        tail = tail + jnp.sum(lrow, axis=-1, keepdims=True)

    hrow = lax.broadcasted_iota(jnp.int32, (HEADS, PAGE), 0)
    s = [jnp.zeros((HEADS, PAGE), F32) for _ in range(n_pages)]
    for h in range(HEADS):
        qc = jnp.broadcast_to(q_t[:, h:h + 1], (DH, PAGE))
        for j in range(n_pages):
            sh = jnp.sum(kp[j][h] * qc, axis=0, keepdims=True)
            s[j] = jnp.where(hrow == h, jnp.broadcast_to(sh, (HEADS, PAGE)), s[j])
    s = [s[j] + bias[j] for j in range(n_pages)]
    s_new = jnp.sum(q * kn, axis=-1, keepdims=True)
    m = s_new
    for j in range(n_pages):
        m = jnp.maximum(m, jnp.max(s[j], axis=-1, keepdims=True))
    pe = [jnp.exp(s[j] - m) for j in range(n_pages)]
    pn = jnp.exp(s_new - m)
    den = pn
    for j in range(n_pages):
        den = den + jnp.sum(pe[j], axis=-1, keepdims=True)
    inv = 1.0 / den
    o_t = jnp.zeros((DH, LANES), F32)
    olane = lax.broadcasted_iota(jnp.int32, (DH, LANES), 1)
    for h in range(HEADS):
        acc = jnp.zeros((DH, PAGE), F32)
        for j in range(n_pages):
            acc = acc + vp[j][h] * jnp.broadcast_to(pe[j][h:h + 1, :], (DH, PAGE))
        col = jnp.sum(acc, axis=-1, keepdims=True)
        o_t = jnp.where(olane == h, jnp.broadcast_to(col, (DH, LANES)), o_t)
    pick_h = jnp.where(e8c == e8r, 1.0, 0.0).astype(BF16)
    o = _sel_dot_nt(pick_h, o_t)
    o_ref[...] = (o + pn * vn) * inv


def _fox_sample(ps3, page_table, layer_arr, cache_k_t, cache_v_t, cache_lf_t, fb_row):
    nb = ps3.shape[0]
    n_pages = page_table.shape[1]
    kern = functools.partial(_fox_sample_kernel, n_pages=n_pages)
    cq0 = C_FOX // HW

    def page_spec(j, tail):
        nz = (0,) * len(tail)
        return pl.BlockSpec((None, None) + tail, lambda b, pt, lay: (lay[0], pt[b, j]) + nz)

    in_specs = [pl.BlockSpec((None, 1, HW), lambda b, pt, lay: (b, 0, cq0)),
                pl.BlockSpec((None, 1, HW), lambda b, pt, lay: (b, 0, cq0 + 1)),
                pl.BlockSpec((None, 1, HW), lambda b, pt, lay: (b, 0, cq0 + 2)),
                pl.BlockSpec((None, 1, LANES), lambda b, pt, lay: (b, 0, C_SMALL // LANES)),
                pl.BlockSpec((1, LANES), lambda b, pt, lay: (0, 0))]
    in_specs += [page_spec(j, (HEADS, DH, PAGE)) for j in range(n_pages)]
    in_specs += [page_spec(j, (HEADS, DH, PAGE)) for j in range(n_pages)]
    in_specs += [page_spec(j, (HEADS, PAGE)) for j in range(n_pages)]
    grid_spec = pltpu.PrefetchScalarGridSpec(
        num_scalar_prefetch=2,
        grid=(nb,),
        in_specs=in_specs,
        out_specs=[pl.BlockSpec((None, HEADS, DH), lambda b, pt, lay: (b, 0, 0)),
                   pl.BlockSpec((None, 1, LANES), lambda b, pt, lay: (b, 0, 0))])
    return pl.pallas_call(
        kern,
        grid_spec=grid_spec,
        out_shape=[jax.ShapeDtypeStruct((nb, HEADS, DH), F32),
                   jax.ShapeDtypeStruct((nb, 1, LANES), F32)],
        compiler_params=_cp(("arbitrary",)),
        name="fox_sample",
    )(page_table, layer_arr, ps3, ps3, ps3, ps3, fb_row,
      *([cache_k_t] * n_pages), *([cache_v_t] * n_pages), *([cache_lf_t] * n_pages))


def _merge_kernel(x_ref, oa_ref, ob_ref, g_ref, wa_ref, wb_ref, wo_ref, lg_ref, lb_ref, y_ref):
    ga = jax.nn.sigmoid(g_ref[:, :D_MODEL])
    gb = jax.nn.sigmoid(g_ref[:, D_MODEL:])
    a = jnp.dot(oa_ref[...].astype(BF16), wa_ref[...], preferred_element_type=F32)
    b = jnp.dot(ob_ref[...].astype(BF16), wb_ref[...], preferred_element_type=F32)
    merged = ga * a + gb * b
    mix = jnp.dot(merged.astype(BF16), wo_ref[...], preferred_element_type=F32)
    y_ref[...] = _layer_norm(ALPHA * x_ref[...] + mix, lg_ref[...], lb_ref[...])


def _merge(x, oa, ob, pp, wa, wb, wo, lg, lb, tm):
    m = x.shape[0]
    const = lambda i: (0, 0)
    return pl.pallas_call(
        _merge_kernel,
        grid=(m // tm,),
        in_specs=[pl.BlockSpec((tm, D_MODEL), lambda i: (i, 0)),
                  pl.BlockSpec((tm, HW), lambda i: (i, 0)),
                  pl.BlockSpec((tm, HW), lambda i: (i, 0)),
                  pl.BlockSpec((tm, 2 * D_MODEL), lambda i: (i, C_GATE // (2 * D_MODEL))),
                  pl.BlockSpec((HW, D_MODEL), const),
                  pl.BlockSpec((HW, D_MODEL), const),
                  pl.BlockSpec((D_MODEL, D_MODEL), const),
                  pl.BlockSpec((1, D_MODEL), const),
                  pl.BlockSpec((1, D_MODEL), const)],
        out_specs=pl.BlockSpec((tm, D_MODEL), lambda i: (i, 0)),
        out_shape=jax.ShapeDtypeStruct((m, D_MODEL), F32),
        compiler_params=_cp(("arbitrary",)),
        name="merge_ln",
    )(x, oa, ob, pp, wa, wb, wo, lg, lb)


def _router_kernel(x_ref, r_ref, c_ref):
    logits = jnp.dot(x_ref[...], r_ref[...], preferred_element_type=F32,
                     precision=lax.Precision.HIGHEST)
    lane = lax.broadcasted_iota(jnp.int32, logits.shape, 1).astype(F32)
    neg = -jnp.inf
    lg = jnp.where(lane < N_EXPERTS, logits, neg)
    m1 = jnp.max(lg, axis=-1, keepdims=True)
    i1 = jnp.min(jnp.where(lg == m1, lane, float(LANES)), axis=-1, keepdims=True)
    lg2 = jnp.where(lane == i1, neg, lg)
    m2 = jnp.max(lg2, axis=-1, keepdims=True)
    i2 = jnp.min(jnp.where(lg2 == m2, lane, float(LANES)), axis=-1, keepdims=True)
    e2 = jnp.exp(m2 - m1)
    den = 1.0 + e2
    c_ref[...] = jnp.where(lane == i1, 1.0 / den, 0.0) + jnp.where(lane == i2, e2 / den, 0.0)


def _router(x, r_pad, tm):
    m = x.shape[0]
    return pl.pallas_call(
        _router_kernel,
        grid=(m // tm,),
        in_specs=[pl.BlockSpec((tm, D_MODEL), lambda i: (i, 0)),
                  pl.BlockSpec((D_MODEL, LANES), lambda i: (0, 0))],
        out_specs=pl.BlockSpec((tm, LANES), lambda i: (i, 0)),
        out_shape=jax.ShapeDtypeStruct((m, LANES), F32),
        compiler_params=_cp(("arbitrary",)),
        name="router",
    )(x, r_pad)


def _ffn_kernel(x_ref, c_ref, wg_ref, wu_ref, wd_ref, lg_ref, lb_ref, y_ref, acc_ref, *, weighted):
    s = pl.program_id(1)
    ns = pl.num_programs(1)
    xb = x_ref[...].astype(BF16)
    hg = jnp.dot(xb, wg_ref[...], preferred_element_type=F32)
    hu = jnp.dot(xb, wu_ref[...], preferred_element_type=F32)
    d = jnp.dot((_silu(hg) * hu).astype(BF16), wd_ref[...], preferred_element_type=F32)
    if weighted:
        c = c_ref[...]
        lane = lax.broadcasted_iota(jnp.int32, c.shape, 1)
        d = jnp.sum(jnp.where(lane == s, c, 0.0), axis=-1, keepdims=True) * d

    @pl.when(s == 0)
    def _():
        acc_ref[...] = d

    @pl.when(s > 0)
    def _():
        acc_ref[...] = acc_ref[...] + d

    @pl.when(s == ns - 1)
    def _():
        y_ref[...] = _layer_norm(ALPHA * x_ref[...] + acc_ref[...], lg_ref[...], lb_ref[...])


def _ffn(x, comb, wg, wu, wd, lg, lb, tm, weighted):
    m = x.shape[0]
    ns = wg.shape[0]
    kern = functools.partial(_ffn_kernel, weighted=weighted)
    return pl.pallas_call(
        kern,
        grid=(m // tm, ns),
        in_specs=[pl.BlockSpec((tm, D_MODEL), lambda i, s: (i, 0)),
                  pl.BlockSpec((tm, LANES), lambda i, s: (i, 0)),
                  pl.BlockSpec((None, D_MODEL, FF_TILE), lambda i, s: (s, 0, 0)),
                  pl.BlockSpec((None, D_MODEL, FF_TILE), lambda i, s: (s, 0, 0)),
                  pl.BlockSpec((None, FF_TILE, D_MODEL), lambda i, s: (s, 0, 0)),
                  pl.BlockSpec((1, D_MODEL), lambda i, s: (0, 0)),
                  pl.BlockSpec((1, D_MODEL), lambda i, s: (0, 0))],
        out_specs=pl.BlockSpec((tm, D_MODEL), lambda i, s: (i, 0)),
        out_shape=jax.ShapeDtypeStruct((m, D_MODEL), F32),
        scratch_shapes=[pltpu.VMEM((tm, D_MODEL), F32)],
        compiler_params=_cp(("arbitrary", "arbitrary")),
        name="ffn_ln",
    )(x, comb, wg, wu, wd, lg, lb)


def _lane_row(vec, offset):
    return jnp.zeros((1, LANES), F32).at[0, offset:offset + vec.shape[0]].set(vec.astype(F32))


def kernel(x_prompt, x_sample, cache_k, cache_v, cache_logf, state_gdn, state_conv, page_table,
           w_in, conv_w, gdn_a_log, gdn_dt_bias, gdn_norm_w, fox_f_bias, w_branch_a, w_branch_b,
           w_out, ln1_g, ln1_b, ln2_g, ln2_b, ffn_w_gate, ffn_w_up, ffn_w_down,
           moe_router, moe_w_gate, moe_w_up, moe_w_down):
    bp, sp, d = x_prompt.shape
    db = x_sample.shape[0]
    depth = w_in.shape[0]
    n_pages = page_table.shape[1]
    mp = bp * sp
    tm_p, tm_s = 512, db

    xp = x_prompt.reshape(mp, d)
    xs = x_sample.reshape(db, d)
    state_conv_t = jnp.swapaxes(state_conv, 1, 2)
    cache_k_t = jnp.transpose(cache_k, (0, 1, 3, 4, 2))
    cache_v_t = jnp.transpose(cache_v, (0, 1, 3, 4, 2))
    cache_lf_t = jnp.transpose(cache_logf, (0, 1, 3, 2))

    o_qkv = CONV_CH + HW
    o_fox = o_qkv + 2 * HEADS
    o_f = o_fox + 3 * HW
    o_gate = o_f + HEADS

    outs = {k: [] for k in ("kp", "vp", "lfp", "gp", "cp", "ks", "vs", "lfs", "gs", "cs")}
    for l in range(depth):
        w = w_in[l]
        w_proj = jnp.concatenate(
            [w[:, 0:o_qkv], w[:, o_gate:o_gate + 2 * d], w[:, o_fox:o_fox + 3 * HW],
             w[:, o_qkv:o_qkv + 2 * HEADS], w[:, o_f:o_f + HEADS],
             jnp.zeros((d, LANES - 3 * HEADS), F32)], axis=1).astype(BF16)
        wa, wb, wo = w_branch_a[l].astype(BF16), w_branch_b[l].astype(BF16), w_out[l].astype(BF16)
        alog_row = _lane_row(gdn_a_log[l], L_A)
        dtb_row = _lane_row(gdn_dt_bias[l], L_A)
        fb_row = _lane_row(fox_f_bias[l], L_F)
        nw_row = jnp.tile(gdn_norm_w[l].astype(F32), 2).reshape(1, LANES)
        cw = conv_w[l]
        lg1, lb1 = ln1_g[l].reshape(1, d), ln1_b[l].reshape(1, d)
        lg2, lb2 = ln2_g[l].reshape(1, d), ln2_b[l].reshape(1, d)

        pp = _proj(xp, w_proj, 1024)
        oa_p, s_p = _gdn_prompt(pp, bp, sp, cw, alog_row, dtb_row, nw_row)
        lf_p, ck_p, cr_p = _fox_prep(pp, bp, sp, fb_row)
        ob_p = _fox_prompt(pp, ck_p, cr_p, bp, sp)
        x1p = _merge(xp, oa_p, ob_p, pp, wa, wb, wo, lg1, lb1, tm_p)

        ps = _proj(xs, w_proj, tm_s)
        oa_s, s_s = _gdn_sample(ps, state_conv_t[l], state_gdn, l, cw, alog_row, dtb_row, nw_row)
        ob_s3, lf_s = _fox_sample(ps.reshape(db, 1, N_PROJ), page_table,
                                  jnp.full((1,), l, jnp.int32), cache_k_t, cache_v_t, cache_lf_t, fb_row)
        x1s = _merge(xs, oa_s, ob_s3.reshape(db, HW), ps, wa, wb, wo, lg1, lb1, tm_s)

        i = l // 2
        if l % 2 == 0:
            def tiles(wm):
                return jnp.swapaxes(wm.reshape(d, -1, FF_TILE), 0, 1).astype(BF16)
            wg, wu = tiles(ffn_w_gate[i]), tiles(ffn_w_up[i])
            wd = ffn_w_down[i].reshape(-1, FF_TILE, d).astype(BF16)
            cmb_p = jnp.zeros((mp, LANES), F32)
            cmb_s = jnp.zeros((db, LANES), F32)
            weighted = False
        else:
            wg, wu, wd = (moe_w_gate[i].astype(BF16), moe_w_up[i].astype(BF16),
                          moe_w_down[i].astype(BF16))
            r_pad = jnp.concatenate([moe_router[i], jnp.zeros((d, LANES - N_EXPERTS), F32)], axis=1)
            cmb_p = _router(x1p, r_pad, tm_p)
            cmb_s = _router(x1s, r_pad, tm_s)
            weighted = True
        xp = _ffn(x1p, cmb_p, wg, wu, wd, lg2, lb2, tm_p, weighted)
        xs = _ffn(x1s, cmb_s, wg, wu, wd, lg2, lb2, tm_s, weighted)

        npp = sp // PAGE
        kcol = C_FOX + HW
        outs["kp"].append(pp[:, kcol:kcol + HW].reshape(bp, npp, PAGE, HEADS, DH))
        outs["vp"].append(pp[:, kcol + HW:kcol + 2 * HW].reshape(bp, npp, PAGE, HEADS, DH))
        outs["lfp"].append(lf_p[:, L_F:L_F + HEADS].reshape(bp, npp, PAGE, HEADS))
        outs["gp"].append(s_p)
        outs["cp"].append(pp.reshape(bp, sp, N_PROJ)[:, sp - (CONV_K - 1):, 0:CONV_CH])
        outs["ks"].append(ps[:, kcol:kcol + HW].reshape(db, 1, HEADS, DH))
        outs["vs"].append(ps[:, kcol + HW:kcol + 2 * HW].reshape(db, 1, HEADS, DH))
        outs["lfs"].append(lf_s[:, :, L_F:L_F + HEADS])
        outs["gs"].append(s_s)
        outs["cs"].append(jnp.concatenate([state_conv[l][:, 1:], ps[:, None, 0:CONV_CH]], axis=1))

    st = {k: jnp.stack(v) for k, v in outs.items()}
    return (xp.reshape(bp, sp, d), xs.reshape(db, 1, d),
            st["kp"], st["vp"], st["lfp"], st["gp"], st["cp"],
            st["ks"], st["vs"], st["lfs"], st["gs"], st["cs"])
```

```python
import functools

import numpy as np
import jax
import jax.numpy as jnp
from jax import lax
from jax.experimental import pallas as pl
from jax.experimental.pallas import tpu as pltpu

F32 = jnp.float32
BF16 = jnp.bfloat16

D_MODEL = 1024
HEADS = 8
DH = 64
HW = HEADS * DH
CONV_K = 4
CONV_CH = 3 * HW
CHUNK = 64
PAGE = 128
N_EXPERTS = 8
FF_TILE = 1408
LANES = 128
N_PAIR = HEADS // 2
GDN_GROUP = 4

DEPTH = 4
ALPHA = (2.0 * DEPTH) ** 0.25
LN_EPS = 1e-5
RMS_EPS = 1e-6
LOG2E = 1.4426950408889634

C_GDN = 0
C_GATE = 2 * D_MODEL
C_FOX = 4 * D_MODEL
C_SMALL = C_FOX + 3 * HW
N_PROJ = C_SMALL + LANES
L_B, L_A, L_F = 0, 8, 16

VMEM_LIMIT = 56 * 1024 * 1024


def _cp(sem, vmem=VMEM_LIMIT):
    return pltpu.CompilerParams(dimension_semantics=sem, vmem_limit_bytes=vmem)


def _bdot(a, b):
    return jnp.dot(a.astype(BF16), b.astype(BF16), preferred_element_type=F32)


def _bdot_nt(a, b):
    return lax.dot_general(a.astype(BF16), b.astype(BF16), (((1,), (1,)), ((), ())),
                           preferred_element_type=F32)


def _bdot_tn(a, b):
    return lax.dot_general(a.astype(BF16), b.astype(BF16), (((0,), (0,)), ((), ())),
                           preferred_element_type=F32)


def _split3(x):
    x1 = x.astype(BF16)
    r1 = x - x1.astype(F32)
    x2 = r1.astype(BF16)
    x3 = (r1 - x2.astype(F32)).astype(BF16)
    return x1, x2, x3


def _sel_dot(sel, x):
    out = None
    for p in _split3(x):
        t = jnp.dot(sel, p, preferred_element_type=F32)
        out = t if out is None else out + t
    return out


def _dot_sel(x, sel):
    out = None
    for p in _split3(x):
        t = jnp.dot(p, sel, preferred_element_type=F32)
        out = t if out is None else out + t
    return out


def _sel_dot_nt(sel, x):
    out = None
    for p in _split3(x):
        t = lax.dot_general(sel, p, (((1,), (1,)), ((), ())), preferred_element_type=F32)
        out = t if out is None else out + t
    return out


def _silu(x):
    return x * jax.nn.sigmoid(x)


def _softplus(x):
    return jnp.maximum(x, 0.0) + jnp.log1p(jnp.exp(-jnp.abs(x)))


def _log_sigmoid(x):
    return jnp.minimum(x, 0.0) - jnp.log1p(jnp.exp(-jnp.abs(x)))


def _layer_norm(h, g, b):
    mu = jnp.mean(h, axis=-1, keepdims=True)
    d = h - mu
    var = jnp.mean(d * d, axis=-1, keepdims=True)
    return d * lax.rsqrt(var + LN_EPS) * g + b


def _stack_heads(x):
    lo = lax.broadcasted_iota(jnp.int32, x.shape, 1) < DH
    return jnp.concatenate([jnp.where(lo, x, 0.0), jnp.where(lo, 0.0, x)], axis=0)


def _fold_heads(x):
    r = x.shape[0] // 2
    return x[:r] + x[r:]


def _proj_kernel(x_ref, w_ref, o_ref):
    o_ref[...] = lax.dot_general(x_ref[...].astype(BF16), w_ref[...], (((1,), (1,)), ((), ())),
                                 preferred_element_type=F32)


def _proj(x, w_t, tm):
    m, d = x.shape
    n = w_t.shape[0]
    tn = 1920
    return pl.pallas_call(
        _proj_kernel,
        grid=(n // tn, m // tm),
        in_specs=[pl.BlockSpec((tm, d), lambda j, i: (i, 0)),
                  pl.BlockSpec((tn, d), lambda j, i: (j, 0))],
        out_specs=pl.BlockSpec((tm, tn), lambda j, i: (i, j)),
        out_shape=jax.ShapeDtypeStruct((m, n), F32),
        compiler_params=_cp(("arbitrary", "arbitrary")),
        name="proj",
    )(x, w_t)


def _kv_t_kernel(x_ref, w_ref, kprev_ref, vprev_ref, k_ref, v_ref, *, tm):
    del kprev_ref, vprev_ref
    kv_t = lax.dot_general(w_ref[...], x_ref[...].astype(BF16), (((1,), (1,)), ((), ())),
                           preferred_element_type=F32)
    for c in range(tm // PAGE):
        cols = slice(c * PAGE, (c + 1) * PAGE)
        k_ref[c] = kv_t[0:HW, cols].reshape(HEADS, DH, PAGE)
        v_ref[c] = kv_t[HW:2 * HW, cols].reshape(HEADS, DH, PAGE)


def _kv_t(x, w_kv_t, k_prev, v_prev, layer, batch, seq, tm=1024):
    nt = seq // tm
    npg = tm // PAGE
    kern = functools.partial(_kv_t_kernel, tm=tm)
    out_spec = pl.BlockSpec((None, None, npg, HEADS, DH, PAGE), lambda b, t: (layer, b, t, 0, 0, 0))
    shape = jax.ShapeDtypeStruct(k_prev.shape, F32)
    return pl.pallas_call(
        kern,
        grid=(batch, nt),
        in_specs=[pl.BlockSpec((tm, D_MODEL), lambda b, t: (b * nt + t, 0)),
                  pl.BlockSpec((2 * HW, D_MODEL), lambda b, t: (0, 0)),
                  pl.BlockSpec(memory_space=pl.ANY),
                  pl.BlockSpec(memory_space=pl.ANY)],
        out_specs=[out_spec, out_spec],
        out_shape=[shape, shape],
        input_output_aliases={2: 0, 3: 1},
        compiler_params=_cp(("arbitrary", "arbitrary")),
        name="kv_t",
    )(x, w_kv_t, k_prev, v_prev)


def _pair_rsqrt_sum(x):
    lo = lax.broadcasted_iota(jnp.int32, x.shape, 1) < DH
    sq = x * x
    s0 = jnp.sum(jnp.where(lo, sq, 0.0), axis=-1, keepdims=True)
    s1 = jnp.sum(jnp.where(lo, 0.0, sq), axis=-1, keepdims=True)
    return jnp.where(lo, lax.rsqrt(s0 + RMS_EPS), lax.rsqrt(s1 + RMS_EPS))


def _gated_norm(o, z, nw):
    lo = lax.broadcasted_iota(jnp.int32, o.shape, 1) < DH
    sq = o * o
    m0 = jnp.sum(jnp.where(lo, sq, 0.0), axis=-1, keepdims=True) * (1.0 / DH)
    m1 = jnp.sum(jnp.where(lo, 0.0, sq), axis=-1, keepdims=True) * (1.0 / DH)
    r = jnp.where(lo, lax.rsqrt(m0 + RMS_EPS), lax.rsqrt(m1 + RMS_EPS))
    return o * r * nw * _silu(z)


def _gdn_prompt_kernel(p_ref, sm_ref, cw_ref, alog_ref, dtb_ref, nw_ref,
                       o_ref, s_out_ref,
                       xbuf, qkv_s, st_s, *, tb):
    t = pl.program_id(1)
    nt = pl.num_programs(1)

    @pl.when(t == 0)
    def _():
        xbuf[0:8, :] = jnp.zeros((8, CONV_CH), F32)
        st_s[...] = jnp.zeros(st_s.shape, F32)

    raw = p_ref[:, 0:CONV_CH]
    xbuf[8:8 + tb, :] = raw
    cw = cw_ref[...]
    y = cw[0:1] * xbuf[5:5 + tb, :] + cw[1:2] * xbuf[6:6 + tb, :]
    y = y + cw[2:3] * xbuf[7:7 + tb, :]
    y = y + cw[3:4] * raw
    xbuf[0:8, :] = raw[tb - 8:tb, :]
    y = _silu(y)
    for p in range(N_PAIR):
        qp = y[:, p * LANES:(p + 1) * LANES]
        qkv_s[:, p * LANES:(p + 1) * LANES] = qp * _pair_rsqrt_sum(qp) * (DH ** -0.5)
        kp = y[:, HW + p * LANES:HW + (p + 1) * LANES]
        qkv_s[:, HW + p * LANES:HW + (p + 1) * LANES] = kp * _pair_rsqrt_sum(kp)
    qkv_s[:, 2 * HW:] = y[:, 2 * HW:]

    c2 = 2 * CHUNK
    ri = lax.broadcasted_iota(jnp.int32, (c2, c2), 0)
    ci = lax.broadcasted_iota(jnp.int32, (c2, c2), 1)
    same = (ri >= CHUNK) == (ci >= CHUNK)
    ti = jnp.where(ri >= CHUNK, ri - CHUNK, ri)
    tj = jnp.where(ci >= CHUNK, ci - CHUNK, ci)
    incl = jnp.logical_and(same, ti >= tj)
    strict = jnp.logical_and(same, ti > tj)
    lr = lax.broadcasted_iota(jnp.int32, (CHUNK, CHUNK), 0)
    lc = lax.broadcasted_iota(jnp.int32, (CHUNK, CHUNK), 1)
    ltri = jnp.where(lr >= lc, 1.0, 0.0).astype(BF16)
    alog = alog_ref[...]
    dtb = dtb_ref[...]
    nw = nw_ref[...]

    def colb(a, lane0, lane1):
        return jnp.concatenate([jnp.broadcast_to(a[:, lane0:lane0 + 1], (CHUNK, LANES)),
                                jnp.broadcast_to(a[:, lane1:lane1 + 1], (CHUNK, LANES))], axis=0)

    def group_body(gi, carry):
        units = [(ci, p) for ci in range(GDN_GROUP) for p in range(N_PAIR)]
        c0s = [pl.multiple_of((gi * GDN_GROUP + ci) * CHUNK, CHUNK) for ci in range(GDN_GROUP)]
        beta_all, gcum = [], []
        for ci in range(GDN_GROUP):
            sm = sm_ref[pl.ds(c0s[ci], CHUNK), :]
            beta_all.append(jax.nn.sigmoid(sm))
            g_all = -jnp.exp(alog) * _softplus(sm + dtb)
            gcum.append(_sel_dot(ltri, g_all))
        q_in = [qkv_s[pl.ds(c0s[ci], CHUNK), p * LANES:(p + 1) * LANES] for ci, p in units]
        k_in = [qkv_s[pl.ds(c0s[ci], CHUNK), HW + p * LANES:HW + (p + 1) * LANES] for ci, p in units]
        v_in = [qkv_s[pl.ds(c0s[ci], CHUNK), 2 * HW + p * LANES:2 * HW + (p + 1) * LANES] for ci, p in units]
        z_in = [p_ref[pl.ds(c0s[ci], CHUNK), CONV_CH + p * LANES:CONV_CH + (p + 1) * LANES] for ci, p in units]
        state = [st_s[p] for p in range(N_PAIR)]
        nu = range(len(units))
        bb, cb, glast = [], [], []
        for ci, p in units:
            h0, h1 = 2 * p, 2 * p + 1
            bb.append(colb(beta_all[ci], L_B + h0, L_B + h1))
            cb.append(colb(gcum[ci], L_A + h0, L_A + h1))
            glast.append(jnp.concatenate(
                [jnp.broadcast_to(gcum[ci][CHUNK - 1:CHUNK, L_A + h0:L_A + h0 + 1], (CHUNK, LANES)),
                 jnp.broadcast_to(gcum[ci][CHUNK - 1:CHUNK, L_A + h1:L_A + h1 + 1], (CHUNK, LANES))], axis=0))
        decay = [jnp.where(incl, jnp.exp(jnp.where(incl, cb[u] - cb[u].T, 0.0)), 0.0) for u in nu]
        eg = [jnp.exp(cb[u]) for u in nu]
        qst = [_stack_heads(q_in[u]) for u in nu]
        kst = [_stack_heads(k_in[u]) for u in nu]
        vst = [_stack_heads(v_in[u]) for u in nu]
        kst_b = [kst[u].astype(BF16) for u in nu]
        kk = [_bdot_nt(kst_b[u], kst_b[u]) for u in nu]
        x = [-jnp.where(strict, bb[u] * kk[u] * decay[u], 0.0) for u in nu]
        pm = x
        for _ in range(5):
            x = [_bdot(x[u], x[u]) for u in nu]
            px = [_bdot(pm[u], x[u]) for u in nu]
            pm = [pm[u] + x[u] + px[u] for u in nu]
        rhs = [jnp.concatenate([vst[u] * bb[u], kst[u] * (bb[u] * eg[u])], axis=1) for u in nu]
        tr = [rhs[u] + _bdot(pm[u], rhs[u]) for u in nu]
        qk = [_bdot_nt(qst[u], kst_b[u]) * decay[u] for u in nu]
        qw = [jnp.concatenate([_fold_heads(qst[u] * eg[u]), _fold_heads(tr[u][:, LANES:])], axis=0) for u in nu]
        kdec = [kst[u] * jnp.exp(glast[u] - cb[u]) for u in nu]
        sdec = [jnp.exp(glast[u]) for u in nu]
        for ci in range(GDN_GROUP):
            us = [ci * N_PAIR + p for p in range(N_PAIR)]
            qws = [_bdot(qw[u], state[p]) for p, u in enumerate(us)]
            ust = [_stack_heads(_fold_heads(tr[u][:, :LANES]) - qws[p][CHUNK:]) for p, u in enumerate(us)]
            o = [qws[p][:CHUNK] + _fold_heads(_bdot(qk[u], ust[p])) for p, u in enumerate(us)]
            state = [state[p] * sdec[u] + _bdot_tn(kdec[u], ust[p]) for p, u in enumerate(us)]
            for p, u in enumerate(us):
                o_ref[pl.ds(c0s[ci], CHUNK), p * LANES:(p + 1) * LANES] = _gated_norm(o[p], z_in[u], nw)
        for p in range(N_PAIR):
            st_s[p] = state[p]
        return carry

    lax.fori_loop(0, tb // (CHUNK * GDN_GROUP), group_body, 0)

    @pl.when(t == nt - 1)
    def _():
        for p in range(N_PAIR):
            s = st_s[p]
            s_out_ref[2 * p] = s[:DH, :DH]
            s_out_ref[2 * p + 1] = s[DH:, DH:]


def _gdn_prompt(pp, batch, seq, conv_w, alog_row, dtb_row, nw_row, tb=512):
    nt = seq // tb
    kern = functools.partial(_gdn_prompt_kernel, tb=tb)
    return pl.pallas_call(
        kern,
        grid=(batch, nt),
        in_specs=[pl.BlockSpec((tb, 2 * D_MODEL), lambda b, t: (b * nt + t, 0)),
                  pl.BlockSpec((tb, LANES), lambda b, t: (b * nt + t, C_SMALL // LANES)),
                  pl.BlockSpec((CONV_K, CONV_CH), lambda b, t: (0, 0)),
                  pl.BlockSpec((1, LANES), lambda b, t: (0, 0)),
                  pl.BlockSpec((1, LANES), lambda b, t: (0, 0)),
                  pl.BlockSpec((1, LANES), lambda b, t: (0, 0))],
        out_specs=[pl.BlockSpec((tb, HW), lambda b, t: (b * nt + t, 0)),
                   pl.BlockSpec((None, HEADS, DH, DH), lambda b, t: (b, 0, 0, 0))],
        out_shape=[jax.ShapeDtypeStruct((batch * seq, HW), F32),
                   jax.ShapeDtypeStruct((batch, HEADS, DH, DH), F32)],
        scratch_shapes=[pltpu.VMEM((tb + 8, CONV_CH), F32),
                        pltpu.VMEM((tb, CONV_CH), F32),
                        pltpu.VMEM((N_PAIR, LANES, LANES), F32)],
        compiler_params=_cp(("arbitrary", "arbitrary")),
        name="gdn_prompt",
    )(pp, pp, conv_w, alog_row, dtb_row, nw_row)


def _gdn_sample_kernel(p_ref, sm_ref, sc_ref, st_ref, prev_ref, cw_ref, alog_ref, dtb_ref, nw_ref,
                       o_ref, s_out_ref, qkv_t, gate_t, o_t):
    del prev_ref
    h = pl.program_id(0)
    nb = p_ref.shape[0]

    @pl.when(h == 0)
    def _():
        raw = p_ref[:, 0:CONV_CH]
        cw = cw_ref[...]
        y = cw[0:1] * sc_ref[0] + cw[1:2] * sc_ref[1]
        y = y + cw[2:3] * sc_ref[2]
        y = y + cw[3:4] * raw
        y = _silu(y)
        for p in range(N_PAIR):
            qp = y[:, p * LANES:(p + 1) * LANES]
            qkv_t[p * LANES:(p + 1) * LANES, :] = (qp * _pair_rsqrt_sum(qp) * (DH ** -0.5)).T
            kp = y[:, HW + p * LANES:HW + (p + 1) * LANES]
            qkv_t[HW + p * LANES:HW + (p + 1) * LANES, :] = (kp * _pair_rsqrt_sum(kp)).T
            qkv_t[2 * HW + p * LANES:2 * HW + (p + 1) * LANES, :] = y[:, 2 * HW + p * LANES:2 * HW + (p + 1) * LANES].T
        sm_t = sm_ref[...].T
        beta = jax.nn.sigmoid(sm_t[L_B:L_B + HEADS])
        dec = jnp.exp(-jnp.exp(alog_ref[...]) * _softplus(sm_t[L_A:L_A + HEADS] + dtb_ref[...]))
        for hh in range(HEADS):
            gate_t[hh, 0:1, :] = beta[hh:hh + 1]
            gate_t[hh, 1:2, :] = dec[hh:hh + 1]

    r0 = pl.multiple_of(h * DH, DH)
    q_h = qkv_t[pl.ds(r0, DH), :]
    k_h = qkv_t[pl.ds(HW + r0, DH), :]
    v_h = qkv_t[pl.ds(2 * HW + r0, DH), :]
    g = gate_t[h]
    beta_b = jnp.broadcast_to(g[0:1], (DH, nb))
    dec_b = jnp.broadcast_to(g[1:2], (DH, nb))
    sk = jnp.zeros((DH, nb), F32)
    for k in range(DH):
        sk = sk + st_ref[k] * jnp.broadcast_to(g[1:2] * k_h[k:k + 1], (DH, nb))
    u = beta_b * (v_h - sk)
    o = jnp.zeros((DH, nb), F32)
    for k in range(DH):
        s_new = st_ref[k] * dec_b + jnp.broadcast_to(k_h[k:k + 1], (DH, nb)) * u
        s_out_ref[k] = s_new
        o = o + s_new * jnp.broadcast_to(q_h[k:k + 1], (DH, nb))
    o_t[pl.ds(r0, DH), :] = o

    @pl.when(h == HEADS - 1)
    def _():
        nw = nw_ref[...]
        for p in range(N_PAIR):
            sl = slice(p * LANES, (p + 1) * LANES)
            z = p_ref[:, CONV_CH + p * LANES:CONV_CH + (p + 1) * LANES]
            o_ref[:, sl] = _gated_norm(o_t[sl, :].T, z, nw)


def _gdn_sample(ps, state_conv_t, state_gdn_t, s_prev, layer, conv_w, alog_b, dtb_b, nw_row):
    nb = ps.shape[0]
    depth = state_gdn_t.shape[0]
    return pl.pallas_call(
        _gdn_sample_kernel,
        grid=(HEADS,),
        in_specs=[pl.BlockSpec((nb, 2 * D_MODEL), lambda h: (0, 0)),
                  pl.BlockSpec((nb, LANES), lambda h: (0, C_SMALL // LANES)),
                  pl.BlockSpec((CONV_K - 1, nb, CONV_CH), lambda h: (0, 0, 0)),
                  pl.BlockSpec((None, None, DH, DH, nb), lambda h: (layer, h, 0, 0, 0)),
                  pl.BlockSpec(memory_space=pl.ANY),
                  pl.BlockSpec((CONV_K, CONV_CH), lambda h: (0, 0)),
                  pl.BlockSpec((HEADS, nb), lambda h: (0, 0)),
                  pl.BlockSpec((HEADS, nb), lambda h: (0, 0)),
                  pl.BlockSpec((1, LANES), lambda h: (0, 0))],
        out_specs=[pl.BlockSpec((nb, HW), lambda h: (0, 0)),
                   pl.BlockSpec((None, None, DH, DH, nb), lambda h: (layer, h, 0, 0, 0))],
        out_shape=[jax.ShapeDtypeStruct((nb, HW), F32),
                   jax.ShapeDtypeStruct((depth, HEADS, DH, DH, nb), F32)],
        scratch_shapes=[pltpu.VMEM((CONV_CH, nb), F32),
                        pltpu.VMEM((HEADS, 8, nb), F32),
                        pltpu.VMEM((HW, nb), F32)],
        input_output_aliases={4: 1},
        compiler_params=_cp(("arbitrary",)),
        name="gdn_sample",
    )(ps, ps, state_conv_t, state_gdn_t, s_prev, conv_w, alog_b, dtb_b, nw_row)


def _fox_prep_kernel(sm_ref, fb_ref, lf_ref, ck_ref, cr_ref, *, seq):
    blk = LANES
    ri = lax.broadcasted_iota(jnp.int32, (blk, blk), 0)
    ci = lax.broadcasted_iota(jnp.int32, (blk, blk), 1)
    ltri = jnp.where(ri >= ci, 1.0, 0.0).astype(BF16)
    r32 = lax.broadcasted_iota(jnp.int32, (4 * 8, blk), 0)
    c32 = lax.broadcasted_iota(jnp.int32, (4 * 8, blk), 1)
    pr, jr = r32 // 8, r32 % 8
    selt = jnp.where(jnp.logical_and(jr < 2, c32 == L_F + 2 * pr + jr), 1.0, 0.0).astype(BF16)
    fb = fb_ref[...]
    carry = jnp.zeros((1, blk), F32)
    for i in range(seq // blk):
        lf = _log_sigmoid(sm_ref[i * blk:(i + 1) * blk, :] + fb)
        lf_ref[i * blk:(i + 1) * blk, :] = lf
        c = _sel_dot(ltri, lf) + carry
        for h in range(HEADS):
            ck_ref[h, i * blk:(i + 1) * blk, :] = jnp.broadcast_to(c[:, L_F + h:L_F + h + 1] * LOG2E, (blk, blk))
        cr_ref[i] = _sel_dot_nt(selt, c)
        carry = c[blk - 1:blk, :]


def _fox_prep(pp, batch, seq, fb_row):
    nk = seq // LANES
    kern = functools.partial(_fox_prep_kernel, seq=seq)
    return pl.pallas_call(
        kern,
        grid=(batch,),
        in_specs=[pl.BlockSpec((seq, LANES), lambda b: (b, C_SMALL // LANES)),
                  pl.BlockSpec((1, LANES), lambda b: (0, 0))],
        out_specs=[pl.BlockSpec((seq, LANES), lambda b: (b, 0)),
                   pl.BlockSpec((None, HEADS, seq, LANES), lambda b: (b, 0, 0, 0)),
                   pl.BlockSpec((None, nk, 4 * 8, LANES), lambda b: (b, 0, 0, 0))],
        out_shape=[jax.ShapeDtypeStruct((batch * seq, LANES), F32),
                   jax.ShapeDtypeStruct((batch, HEADS, seq, LANES), F32),
                   jax.ShapeDtypeStruct((batch, nk, 4 * 8, LANES), F32)],
        compiler_params=_cp(("arbitrary",)),
        name="fox_prep",
    )(pp, fb_row)


def _fox_prompt_kernel(q_ref, k_ref, v_ref, ck_ref, cr_ref, o_ref, m_s, l_s, acc_s, *, tq):
    qi = pl.program_id(2)
    lane = lax.broadcasted_iota(jnp.int32, (tq, LANES), 1)
    lo = lane < DH
    q = q_ref[...] * (DH ** -0.5 * LOG2E)
    qm = [jnp.where(lo, q, 0.0).astype(BF16), jnp.where(lo, 0.0, q).astype(BF16)]
    sub = tq // LANES
    cq = [jnp.concatenate([cr_ref[qi * sub + u, h:h + 1, :] for u in range(sub)], axis=1) * LOG2E
          for h in range(2)]
    m_s[...] = jnp.full(m_s.shape, -jnp.inf, F32)
    l_s[...] = jnp.zeros(l_s.shape, F32)
    acc_s[...] = jnp.zeros(acc_s.shape, F32)
    keyi = lax.broadcasted_iota(jnp.int32, (tq, tq), 0)
    qryi = lax.broadcasted_iota(jnp.int32, (tq, tq), 1)

    def steps(js, masked):
        units = [(bi, h) for bi in range(len(js)) for h in range(2)]
        j0s = [pl.multiple_of(j * tq, tq) for j in js]
        kb = [k_ref[pl.ds(j0, tq), :].astype(BF16) for j0 in j0s]
        vt = [v_ref[pl.ds(j0, tq), :].T.astype(BF16) for j0 in j0s]
        ck = [[ck_ref[h, pl.ds(j0, tq), :] for h in range(2)] for j0 in j0s]

        def scores(u):
            bi, h = u
            s = _bdot_nt(kb[bi], qm[h])
            s = s - jnp.concatenate([ck[bi][h]] * sub, axis=1)
            if masked and bi == len(js) - 1:
                s = jnp.where(qryi >= keyi, s, -jnp.inf)
            return s

        def softmax_pv(u, s):
            bi, h = u
            m_old = m_s[h]
            m_new = jnp.maximum(m_old, jnp.max(s, axis=0, keepdims=True) + cq[h])
            alpha = jnp.exp2(m_old - m_new)
            pexp = jnp.exp2(s - (m_new - cq[h]))
            l_s[h] = alpha * l_s[h] + jnp.sum(pexp, axis=0, keepdims=True)
            acc_s[h] = alpha * acc_s[h] + jnp.dot(vt[bi], pexp.astype(BF16), preferred_element_type=F32)
            m_s[h] = m_new

        s_next = scores(units[0])
        for i, u in enumerate(units):
            s_cur = s_next
            if i + 1 < len(units):
                s_next = scores(units[i + 1])
            softmax_pv(u, s_cur)

    def body(jj, carry):
        steps([2 * jj, 2 * jj + 1], False)
        return carry

    lax.fori_loop(0, qi // 2, body, 0)

    @pl.when(qi % 2 == 1)
    def _():
        steps([qi - 1, qi], True)

    @pl.when(qi % 2 == 0)
    def _():
        steps([qi], True)

    top = lax.broadcasted_iota(jnp.int32, (LANES, tq), 0) < DH
    o_t = jnp.where(top, acc_s[0] / l_s[0], acc_s[1] / l_s[1])
    o_ref[...] = o_t.T


def _fox_prompt(pp, ck, cr, batch, seq, tq=512):
    nq = seq // tq
    nk = seq // LANES
    kern = functools.partial(_fox_prompt_kernel, tq=tq)
    cq0 = C_FOX // LANES
    return pl.pallas_call(
        kern,
        grid=(batch, N_PAIR, nq),
        in_specs=[pl.BlockSpec((tq, LANES), lambda b, p, i: (b * nq + i, cq0 + p)),
                  pl.BlockSpec((seq, LANES), lambda b, p, i: (b, cq0 + N_PAIR + p)),
                  pl.BlockSpec((seq, LANES), lambda b, p, i: (b, cq0 + 2 * N_PAIR + p)),
                  pl.BlockSpec((None, 2, seq, LANES), lambda b, p, i: (b, p, 0, 0)),
                  pl.BlockSpec((None, nk, 8, LANES), lambda b, p, i: (b, 0, p, 0))],
        out_specs=pl.BlockSpec((tq, LANES), lambda b, p, i: (b * nq + i, p)),
        out_shape=jax.ShapeDtypeStruct((batch * seq, HW), F32),
        scratch_shapes=[pltpu.VMEM((2, 1, tq), F32), pltpu.VMEM((2, 1, tq), F32),
                        pltpu.VMEM((2, LANES, tq), F32)],
        compiler_params=_cp(("arbitrary", "arbitrary", "arbitrary")),
        name="fox_prompt",
    )(pp, pp, pp, ck, cr)


def _fox_sample_kernel(pt_ref, lay_ref, q_ref, k_ref, v_ref, sm_ref, fb_ref, *rest, n_pages):
    kp = rest[0:n_pages]
    vp = rest[n_pages:2 * n_pages]
    lp = rest[2 * n_pages:3 * n_pages]
    o_ref, lf_ref = rest[3 * n_pages], rest[3 * n_pages + 1]
    del pt_ref, lay_ref

    r8 = lax.broadcasted_iota(jnp.int32, (HEADS, HW), 0)
    c8 = lax.broadcasted_iota(jnp.int32, (HEADS, HW), 1)
    own = (c8 // DH) == r8
    fr = lax.broadcasted_iota(jnp.int32, (HW, DH), 0)
    fc = lax.broadcasted_iota(jnp.int32, (HW, DH), 1)
    fold = jnp.where(fr % DH == fc, 1.0, 0.0).astype(BF16)

    def heads_form(row):
        return _dot_sel(jnp.where(own, jnp.broadcast_to(row, (HEADS, HW)), 0.0), fold)

    q = heads_form(q_ref[...]) * (DH ** -0.5)
    kn = heads_form(k_ref[...])
    vn = heads_form(v_ref[...])
    er = lax.broadcasted_iota(jnp.int32, (DH, DH), 0)
    ec = lax.broadcasted_iota(jnp.int32, (DH, DH), 1)
    eye = jnp.where(er == ec, 1.0, 0.0).astype(BF16)
    q_t = _sel_dot_nt(eye, q)

    lf_new = _log_sigmoid(sm_ref[...] + fb_ref[...])
    lf_ref[...] = lf_new
    e8r = lax.broadcasted_iota(jnp.int32, (HEADS, LANES), 0)
    e8c = lax.broadcasted_iota(jnp.int32, (HEADS, LANES), 1)
    pick_f = jnp.where(e8c == e8r + L_F, 1.0, 0.0).astype(BF16)
    c_new = _sel_dot_nt(pick_f, jnp.broadcast_to(lf_new, (HEADS, LANES)))[:, 0:1]

    ur = lax.broadcasted_iota(jnp.int32, (PAGE, PAGE), 0)
    uc = lax.broadcasted_iota(jnp.int32, (PAGE, PAGE), 1)
    upper = jnp.where(ur > uc, 1.0, 0.0).astype(BF16)
    bias = [None] * n_pages
    tail = c_new
    for j in reversed(range(n_pages)):
        lrow = lp[j][...]
        bias[j] = _dot_sel(lrow, upper) + tail
        tail = tail + jnp.sum(lrow, axis=-1, keepdims=True)

    hrow = lax.broadcasted_iota(jnp.int32, (HEADS, PAGE), 0)
    s = [jnp.zeros((HEADS, PAGE), F32) for _ in range(n_pages)]
    for h in range(HEADS):
        qc = jnp.broadcast_to(q_t[:, h:h + 1], (DH, PAGE))
        for j in range(n_pages):
            sh = jnp.sum(kp[j][h] * qc, axis=0, keepdims=True)
            s[j] = jnp.where(hrow == h, jnp.broadcast_to(sh, (HEADS, PAGE)), s[j])
    s = [s[j] + bias[j] for j in range(n_pages)]
    s_new = jnp.sum(q * kn, axis=-1, keepdims=True)
    m = s_new
    for j in range(n_pages):
        m = jnp.maximum(m, jnp.max(s[j], axis=-1, keepdims=True))
    pe = [jnp.exp(s[j] - m) for j in range(n_pages)]
    pn = jnp.exp(s_new - m)
    den = pn
    for j in range(n_pages):
        den = den + jnp.sum(pe[j], axis=-1, keepdims=True)
    inv = 1.0 / den
    o_t = jnp.zeros((DH, LANES), F32)
    olane = lax.broadcasted_iota(jnp.int32, (DH, LANES), 1)
    for h in range(HEADS):
        acc = jnp.zeros((DH, PAGE), F32)
        for j in range(n_pages):
            acc = acc + vp[j][h] * jnp.broadcast_to(pe[j][h:h + 1, :], (DH, PAGE))
        col = jnp.sum(acc, axis=-1, keepdims=True)
        o_t = jnp.where(olane == h, jnp.broadcast_to(col, (DH, LANES)), o_t)
    pick_h = jnp.where(e8c == e8r, 1.0, 0.0).astype(BF16)
    o = _sel_dot_nt(pick_h, o_t)
    o_ref[...] = (o + pn * vn) * inv


def _fox_sample(ps3, page_table, layer_arr, cache_k_t, cache_v_t, cache_lf_t, fb_row):
    nb = ps3.shape[0]
    n_pages = page_table.shape[1]
    kern = functools.partial(_fox_sample_kernel, n_pages=n_pages)
    cq0 = C_FOX // HW

    def page_spec(j, tail):
        nz = (0,) * len(tail)
        return pl.BlockSpec((None, None) + tail, lambda b, pt, lay: (lay[0], pt[b, j]) + nz)

    in_specs = [pl.BlockSpec((None, 1, HW), lambda b, pt, lay: (b, 0, cq0)),
                pl.BlockSpec((None, 1, HW), lambda b, pt, lay: (b, 0, cq0 + 1)),
                pl.BlockSpec((None, 1, HW), lambda b, pt, lay: (b, 0, cq0 + 2)),
                pl.BlockSpec((None, 1, LANES), lambda b, pt, lay: (b, 0, C_SMALL // LANES)),
                pl.BlockSpec((1, LANES), lambda b, pt, lay: (0, 0))]
    in_specs += [page_spec(j, (HEADS, DH, PAGE)) for j in range(n_pages)]
    in_specs += [page_spec(j, (HEADS, DH, PAGE)) for j in range(n_pages)]
    in_specs += [page_spec(j, (HEADS, PAGE)) for j in range(n_pages)]
    grid_spec = pltpu.PrefetchScalarGridSpec(
        num_scalar_prefetch=2,
        grid=(nb,),
        in_specs=in_specs,
        out_specs=[pl.BlockSpec((None, HEADS, DH), lambda b, pt, lay: (b, 0, 0)),
                   pl.BlockSpec((None, 1, LANES), lambda b, pt, lay: (b, 0, 0))])
    return pl.pallas_call(
        kern,
        grid_spec=grid_spec,
        out_shape=[jax.ShapeDtypeStruct((nb, HEADS, DH), F32),
                   jax.ShapeDtypeStruct((nb, 1, LANES), F32)],
        compiler_params=_cp(("arbitrary",)),
        name="fox_sample",
    )(page_table, layer_arr, ps3, ps3, ps3, ps3, fb_row,
      *([cache_k_t] * n_pages), *([cache_v_t] * n_pages), *([cache_lf_t] * n_pages))


def _merge_kernel(x_ref, oa_ref, ob_ref, g_ref, wa_ref, wb_ref, wo_ref, lg_ref, lb_ref, y_ref):
    ga = jax.nn.sigmoid(g_ref[:, :D_MODEL])
    gb = jax.nn.sigmoid(g_ref[:, D_MODEL:])
    a = jnp.dot(oa_ref[...].astype(BF16), wa_ref[...], preferred_element_type=F32)
    b = jnp.dot(ob_ref[...].astype(BF16), wb_ref[...], preferred_element_type=F32)
    merged = ga * a + gb * b
    mix = jnp.dot(merged.astype(BF16), wo_ref[...], preferred_element_type=F32)
    y_ref[...] = _layer_norm(ALPHA * x_ref[...] + mix, lg_ref[...], lb_ref[...])


def _merge(x, oa, ob, pp, wa, wb, wo, lg, lb, tm):
    m = x.shape[0]
    const = lambda i: (0, 0)
    return pl.pallas_call(
        _merge_kernel,
        grid=(m // tm,),
        in_specs=[pl.BlockSpec((tm, D_MODEL), lambda i: (i, 0)),
                  pl.BlockSpec((tm, HW), lambda i: (i, 0)),
                  pl.BlockSpec((tm, HW), lambda i: (i, 0)),
                  pl.BlockSpec((tm, 2 * D_MODEL), lambda i: (i, C_GATE // (2 * D_MODEL))),
                  pl.BlockSpec((HW, D_MODEL), const),
                  pl.BlockSpec((HW, D_MODEL), const),
                  pl.BlockSpec((D_MODEL, D_MODEL), const),
                  pl.BlockSpec((1, D_MODEL), const),
                  pl.BlockSpec((1, D_MODEL), const)],
        out_specs=pl.BlockSpec((tm, D_MODEL), lambda i: (i, 0)),
        out_shape=jax.ShapeDtypeStruct((m, D_MODEL), F32),
        compiler_params=_cp(("arbitrary",)),
        name="merge_ln",
    )(x, oa, ob, pp, wa, wb, wo, lg, lb)


def _router_kernel(x_ref, r_ref, c_ref):
    logits = jnp.dot(x_ref[...], r_ref[...], preferred_element_type=F32,
                     precision=lax.Precision.HIGHEST)
    lane = lax.broadcasted_iota(jnp.int32, logits.shape, 1).astype(F32)
    neg = -jnp.inf
    lg = jnp.where(lane < N_EXPERTS, logits, neg)
    m1 = jnp.max(lg, axis=-1, keepdims=True)
    i1 = jnp.min(jnp.where(lg == m1, lane, float(LANES)), axis=-1, keepdims=True)
    lg2 = jnp.where(lane == i1, neg, lg)
    m2 = jnp.max(lg2, axis=-1, keepdims=True)
    i2 = jnp.min(jnp.where(lg2 == m2, lane, float(LANES)), axis=-1, keepdims=True)
    e2 = jnp.exp(m2 - m1)
    den = 1.0 + e2
    c_ref[...] = jnp.where(lane == i1, 1.0 / den, 0.0) + jnp.where(lane == i2, e2 / den, 0.0)


def _router(x, r_pad, tm):
    m = x.shape[0]
    return pl.pallas_call(
        _router_kernel,
        grid=(m // tm,),
        in_specs=[pl.BlockSpec((tm, D_MODEL), lambda i: (i, 0)),
                  pl.BlockSpec((D_MODEL, LANES), lambda i: (0, 0))],
        out_specs=pl.BlockSpec((tm, LANES), lambda i: (i, 0)),
        out_shape=jax.ShapeDtypeStruct((m, LANES), F32),
        compiler_params=_cp(("arbitrary",)),
        name="router",
    )(x, r_pad)


def _ffn_kernel(x_ref, c_ref, wg_ref, wu_ref, wd_ref, lg_ref, lb_ref, y_ref, acc_ref, *, weighted):
    s = pl.program_id(1)
    ns = pl.num_programs(1)
    xb = x_ref[...].astype(BF16)
    hg = jnp.dot(xb, wg_ref[...], preferred_element_type=F32)
    hu = jnp.dot(xb, wu_ref[...], preferred_element_type=F32)
    d = jnp.dot((_silu(hg) * hu).astype(BF16), wd_ref[...], preferred_element_type=F32)
    if weighted:
        c = c_ref[...]
        lane = lax.broadcasted_iota(jnp.int32, c.shape, 1)
        d = jnp.sum(jnp.where(lane == s, c, 0.0), axis=-1, keepdims=True) * d

    @pl.when(s == 0)
    def _():
        acc_ref[...] = d

    @pl.when(s > 0)
    def _():
        acc_ref[...] = acc_ref[...] + d

    @pl.when(s == ns - 1)
    def _():
        y_ref[...] = _layer_norm(ALPHA * x_ref[...] + acc_ref[...], lg_ref[...], lb_ref[...])


def _ffn(x, comb, wg, wu, wd, lg, lb, tm, weighted):
    m = x.shape[0]
    ns = wg.shape[0]
    kern = functools.partial(_ffn_kernel, weighted=weighted)
    return pl.pallas_call(
        kern,
        grid=(m // tm, ns),
        in_specs=[pl.BlockSpec((tm, D_MODEL), lambda i, s: (i, 0)),
                  pl.BlockSpec((tm, LANES), lambda i, s: (i, 0)),
                  pl.BlockSpec((None, D_MODEL, FF_TILE), lambda i, s: (s, 0, 0)),
                  pl.BlockSpec((None, D_MODEL, FF_TILE), lambda i, s: (s, 0, 0)),
                  pl.BlockSpec((None, FF_TILE, D_MODEL), lambda i, s: (s, 0, 0)),
                  pl.BlockSpec((1, D_MODEL), lambda i, s: (0, 0)),
                  pl.BlockSpec((1, D_MODEL), lambda i, s: (0, 0))],
        out_specs=pl.BlockSpec((tm, D_MODEL), lambda i, s: (i, 0)),
        out_shape=jax.ShapeDtypeStruct((m, D_MODEL), F32),
        scratch_shapes=[pltpu.VMEM((tm, D_MODEL), F32)],
        compiler_params=_cp(("arbitrary", "arbitrary")),
        name="ffn_ln",
    )(x, comb, wg, wu, wd, lg, lb)


def _lane_row(vec, offset):
    return jnp.zeros((1, LANES), F32).at[0, offset:offset + vec.shape[0]].set(vec.astype(F32))


def kernel(x_prompt, x_sample, cache_k, cache_v, cache_logf, state_gdn, state_conv, page_table,
           w_in, conv_w, gdn_a_log, gdn_dt_bias, gdn_norm_w, fox_f_bias, w_branch_a, w_branch_b,
           w_out, ln1_g, ln1_b, ln2_g, ln2_b, ffn_w_gate, ffn_w_up, ffn_w_down,
           moe_router, moe_w_gate, moe_w_up, moe_w_down):
    bp, sp, d = x_prompt.shape
    db = x_sample.shape[0]
    depth = w_in.shape[0]
    n_pages = page_table.shape[1]
    mp = bp * sp
    tm_p, tm_s = 512, db

    xp = x_prompt.reshape(mp, d)
    xs = x_sample.reshape(db, d)
    state_conv_t = jnp.swapaxes(state_conv, 1, 2)
    cache_k_t = jnp.transpose(cache_k, (0, 1, 3, 4, 2))
    cache_v_t = jnp.transpose(cache_v, (0, 1, 3, 4, 2))
    cache_lf_t = jnp.transpose(cache_logf, (0, 1, 3, 2))

    o_qkv = CONV_CH + HW
    o_fox = o_qkv + 2 * HEADS
    o_f = o_fox + 3 * HW
    o_gate = o_f + HEADS

    state_gdn_t = jnp.transpose(state_gdn, (0, 2, 3, 4, 1))
    gdn_s_t = jnp.zeros(state_gdn_t.shape, F32)
    npp = sp // PAGE
    k_buf = jnp.zeros((depth, bp, npp, HEADS, DH, PAGE), F32)
    v_buf = jnp.zeros((depth, bp, npp, HEADS, DH, PAGE), F32)
    outs = {k: [] for k in ("lfp", "gp", "cp", "ks", "vs", "lfs", "cs")}
    for l in range(depth):
        wt = jnp.swapaxes(w_in[l], 0, 1)
        w_proj = jnp.concatenate(
            [wt[0:o_qkv], wt[o_gate:o_gate + 2 * d], wt[o_fox:o_fox + 3 * HW],
             wt[o_qkv:o_qkv + 2 * HEADS], wt[o_f:o_f + HEADS],
             jnp.zeros((LANES - 3 * HEADS, d), F32)], axis=0).astype(BF16)
        w_kv_t = wt[o_fox + HW:o_fox + 3 * HW].astype(BF16)
        wa, wb, wo = w_branch_a[l].astype(BF16), w_branch_b[l].astype(BF16), w_out[l].astype(BF16)
        alog_row = _lane_row(gdn_a_log[l], L_A)
        dtb_row = _lane_row(gdn_dt_bias[l], L_A)
        fb_row = _lane_row(fox_f_bias[l], L_F)
        nw_row = jnp.tile(gdn_norm_w[l].astype(F32), 2).reshape(1, LANES)
        cw = conv_w[l]
        lg1, lb1 = ln1_g[l].reshape(1, d), ln1_b[l].reshape(1, d)
        lg2, lb2 = ln2_g[l].reshape(1, d), ln2_b[l].reshape(1, d)

        pp = _proj(xp, w_proj, 1024)
        k_buf, v_buf = _kv_t(xp, w_kv_t, k_buf, v_buf, l, bp, sp)
        oa_p, s_p = _gdn_prompt(pp, bp, sp, cw, alog_row, dtb_row, nw_row)
        lf_p, ck_p, cr_p = _fox_prep(pp, bp, sp, fb_row)
        ob_p = _fox_prompt(pp, ck_p, cr_p, bp, sp)
        x1p = _merge(xp, oa_p, ob_p, pp, wa, wb, wo, lg1, lb1, tm_p)

        ps = _proj(xs, w_proj, tm_s)
        alog_b = jnp.broadcast_to(gdn_a_log[l].astype(F32)[:, None], (HEADS, db))
        dtb_b = jnp.broadcast_to(gdn_dt_bias[l].astype(F32)[:, None], (HEADS, db))
        oa_s, gdn_s_t = _gdn_sample(ps, state_conv_t[l], state_gdn_t, gdn_s_t, l, cw, alog_b, dtb_b, nw_row)
        ob_s3, lf_s = _fox_sample(ps.reshape(db, 1, N_PROJ), page_table,
                                  jnp.full((1,), l, jnp.int32), cache_k_t, cache_v_t, cache_lf_t, fb_row)
        x1s = _merge(xs, oa_s, ob_s3.reshape(db, HW), ps, wa, wb, wo, lg1, lb1, tm_s)

        i = l // 2
        if l % 2 == 0:
            def tiles(wm):
                return jnp.swapaxes(wm.reshape(d, -1, FF_TILE), 0, 1).astype(BF16)
            wg, wu = tiles(ffn_w_gate[i]), tiles(ffn_w_up[i])
            wd = ffn_w_down[i].reshape(-1, FF_TILE, d).astype(BF16)
            cmb_p = jnp.zeros((mp, LANES), F32)
            cmb_s = jnp.zeros((db, LANES), F32)
            weighted = False
        else:
            wg, wu, wd = (moe_w_gate[i].astype(BF16), moe_w_up[i].astype(BF16),
                          moe_w_down[i].astype(BF16))
            r_pad = jnp.concatenate([moe_router[i], jnp.zeros((d, LANES - N_EXPERTS), F32)], axis=1)
            cmb_p = _router(x1p, r_pad, tm_p)
            cmb_s = _router(x1s, r_pad, tm_s)
            weighted = True
        xp = _ffn(x1p, cmb_p, wg, wu, wd, lg2, lb2, tm_p, weighted)
        xs = _ffn(x1s, cmb_s, wg, wu, wd, lg2, lb2, tm_s, weighted)

        kcol = C_FOX + HW
        outs["lfp"].append(lf_p[:, L_F:L_F + HEADS].reshape(bp, npp, PAGE, HEADS))
        outs["gp"].append(s_p)
        outs["cp"].append(pp.reshape(bp, sp, N_PROJ)[:, sp - (CONV_K - 1):, 0:CONV_CH])
        outs["ks"].append(ps[:, kcol:kcol + HW].reshape(db, 1, HEADS, DH))
        outs["vs"].append(ps[:, kcol + HW:kcol + 2 * HW].reshape(db, 1, HEADS, DH))
        outs["lfs"].append(lf_s[:, :, L_F:L_F + HEADS])
        outs["cs"].append(jnp.concatenate([state_conv[l][:, 1:], ps[:, None, 0:CONV_CH]], axis=1))

    st = {k: jnp.stack(v) for k, v in outs.items()}
    return (xp.reshape(bp, sp, d), xs.reshape(db, 1, d),
            jnp.transpose(k_buf, (0, 1, 2, 5, 3, 4)), jnp.transpose(v_buf, (0, 1, 2, 5, 3, 4)),
            st["lfp"], st["gp"], st["cp"],
            st["ks"], st["vs"], st["lfs"], jnp.transpose(gdn_s_t, (0, 4, 1, 2, 3)), st["cs"])
```
